```python
import math
import jax, jax.numpy as jnp
from jax import lax
import numpy as np

D_MODEL = 2048
BATCH = 2
SEQ = 16384
DEPTH = 2

S5_WIDTH = D_MODEL // 2
S5_GROUP = 16
S5_GROUPS = S5_WIDTH // S5_GROUP
S5_STATE = 64
DA_HEADS = 8
DA_HEAD_DIM = 64
DA_V_DIM = 2 * DA_HEAD_DIM
DA_WIDTH = DA_HEADS * DA_V_DIM
E_IN = 2 * S5_WIDTH + 4 * DA_WIDTH
E_OUT = S5_WIDTH + DA_WIDTH
FOX_HEADS = 16
FOX_HEAD_DIM = D_MODEL // FOX_HEADS
FOX_WIDTH = FOX_HEADS * FOX_HEAD_DIM
O_IN = 4 * FOX_WIDTH + FOX_HEADS
Q_BLOCK = 128
N_EVEN = (DEPTH + 1) // 2
N_ODD = DEPTH // 2
EPS = 1e-6

kernel_name = "hybrid_s5_diffattn_fox_block"


def _rms(x, g):
    xf = x.astype(jnp.float32)
    r = lax.rsqrt(jnp.mean(xf * xf, axis=-1, keepdims=True) + EPS)
    return (xf * r * g.astype(jnp.float32)).astype(x.dtype)


def _alibi_slopes(n):
    return np.array([2.0 ** (-8.0 * (h + 1) / n) for h in range(n)], dtype=np.float32)


def _block_sweep(block_fn, seq):
    nb = seq // Q_BLOCK
    out = lax.map(block_fn, jnp.arange(nb))
    nbk, bsz, h, qb, dv = out.shape
    return out.transpose(1, 0, 3, 2, 4).reshape(bsz, seq, h * dv)


def _complex_linear_combine(e1, e2):
    a1r, a1i, b1r, b1i = e1
    a2r, a2i, b2r, b2i = e2
    ar = a1r * a2r - a1i * a2i
    ai = a1r * a2i + a1i * a2r
    br = a2r * b1r - a2i * b1i + b2r
    bi = a2r * b1i + a2i * b1r + b2i
    return (ar, ai, br, bi)


def _s5_mixer(u, lam_re, lam_im, log_dt, b_re, b_im, c_re, c_im, d, w_glu, b_glu):
    bsz, seq, _ = u.shape
    f32 = jnp.float32
    uf = u.astype(f32).reshape(bsz, seq, S5_GROUPS, S5_GROUP)
    dt = jnp.exp(log_dt.astype(f32))[:, None]
    lr = lam_re.astype(f32)
    li = lam_im.astype(f32)
    mag = jnp.exp(lr * dt)
    ar = mag * jnp.cos(li * dt)
    ai = mag * jnp.sin(li * dt)
    den = lr * lr + li * li
    nr = ar - 1.0
    ni = ai
    kr = (nr * lr + ni * li) / den
    ki = (ni * lr - nr * li) / den
    br = b_re.astype(f32)
    bi = b_im.astype(f32)
    bbr = kr[..., None] * br - ki[..., None] * bi
    bbi = kr[..., None] * bi + ki[..., None] * br
    xr = jnp.einsum('blgh,gph->blgp', uf, bbr)
    xi = jnp.einsum('blgh,gph->blgp', uf, bbi)
    a_r = jnp.broadcast_to(ar, (1, seq, S5_GROUPS, S5_STATE))
    a_i = jnp.broadcast_to(ai, (1, seq, S5_GROUPS, S5_STATE))
    _, _, hr, hi = lax.associative_scan(_complex_linear_combine, (a_r, a_i, xr, xi), axis=1)
    y = (jnp.einsum('blgp,ghp->blgh', hr, c_re.astype(f32))
         - jnp.einsum('blgp,ghp->blgh', hi, c_im.astype(f32)))
    y = y.reshape(bsz, seq, S5_WIDTH) + d.astype(f32) * u.astype(f32)
    y = jax.nn.gelu(y)
    y = y * jax.nn.sigmoid(y @ w_glu.astype(f32) + b_glu.astype(f32))
    return y.astype(u.dtype)


def _diff_attention(q, k, v, q_gain, k_gain, lq1, lk1, lq2, lk2, out_gain, lambda_init):
    bsz, seq = q.shape[:2]
    q = _rms(q.reshape(bsz, seq, DA_HEADS, 2, DA_HEAD_DIM), q_gain.reshape(2, DA_HEAD_DIM))
    k = _rms(k.reshape(bsz, seq, DA_HEADS, 2, DA_HEAD_DIM), k_gain.reshape(2, DA_HEAD_DIM))
    q = q.transpose(0, 2, 3, 1, 4)
    k = k.transpose(0, 2, 3, 1, 4)
    v = v.reshape(bsz, seq, DA_HEADS, DA_V_DIM).transpose(0, 2, 1, 3)
    f32 = jnp.float32
    lam = (jnp.exp(jnp.sum(lq1.astype(f32) * lk1.astype(f32)))
           - jnp.exp(jnp.sum(lq2.astype(f32) * lk2.astype(f32))) + lambda_init)
    slopes = jnp.asarray(_alibi_slopes(DA_HEADS))[:, None, None, None]
    scale = DA_HEAD_DIM ** -0.5
    kpos = jnp.arange(seq)

    def blk(i):
        start = i * Q_BLOCK
        qb = lax.dynamic_slice_in_dim(q, start, Q_BLOCK, axis=3)
        s = jnp.einsum('bhcqd,bhckd->bhcqk', qb, k).astype(f32) * scale
        dist = (start + jnp.arange(Q_BLOCK))[:, None] - kpos[None, :]
        s = s - slopes * dist.astype(f32)
        s = jnp.where(dist >= 0, s, -jnp.inf)
        p = jax.nn.softmax(s, axis=-1)
        p = p[:, :, 0] - lam * p[:, :, 1]
        return jnp.einsum('bhqk,bhkd->bhqd', p.astype(v.dtype), v)

    o = _block_sweep(blk, seq).reshape(bsz, seq, DA_HEADS, DA_V_DIM)
    o = _rms(o, out_gain) * (1.0 - lambda_init)
    return o.reshape(bsz, seq, DA_WIDTH)


def _forgetting_attention(q, k, v, f_logit, b_f, q_gain, k_gain):
    bsz, seq = q.shape[:2]
    f32 = jnp.float32
    q = _rms(q.reshape(bsz, seq, FOX_HEADS, FOX_HEAD_DIM), q_gain).transpose(0, 2, 1, 3)
    k = _rms(k.reshape(bsz, seq, FOX_HEADS, FOX_HEAD_DIM), k_gain).transpose(0, 2, 1, 3)
    v = v.reshape(bsz, seq, FOX_HEADS, FOX_HEAD_DIM).transpose(0, 2, 1, 3)
    logf = jax.nn.log_sigmoid(f_logit.astype(f32) + b_f.astype(f32))
    c = jnp.cumsum(logf, axis=1).transpose(0, 2, 1)
    scale = FOX_HEAD_DIM ** -0.5
    kpos = jnp.arange(seq)

    def blk(i):
        start = i * Q_BLOCK
        qb = lax.dynamic_slice_in_dim(q, start, Q_BLOCK, axis=2)
        cq = lax.dynamic_slice_in_dim(c, start, Q_BLOCK, axis=2)
        s = jnp.einsum('bhqd,bhkd->bhqk', qb, k).astype(f32) * scale
        s = s + cq[..., :, None] - c[..., None, :]
        causal = (start + jnp.arange(Q_BLOCK))[:, None] >= kpos[None, :]
        s = jnp.where(causal, s, -jnp.inf)
        p = jax.nn.softmax(s, axis=-1)
        return jnp.einsum('bhqk,bhkd->bhqd', p.astype(v.dtype), v)

    return _block_sweep(blk, seq)


def setup_inputs(seed: int = 0) -> dict:
    key = jax.random.key(seed)
    ks = iter(jax.random.split(key, 40))
    f32 = jnp.float32

    def nrm(shape, scale):
        return jax.random.normal(next(ks), shape, f32) * scale

    ne, no, g, p, gs = N_EVEN, N_ODD, S5_GROUPS, S5_STATE, S5_GROUP
    x = nrm((BATCH, SEQ, D_MODEL), 1.0)
    e_norm = 1.0 + nrm((ne, D_MODEL), 0.02)
    e_w_in = nrm((ne, D_MODEL, E_IN), D_MODEL ** -0.5)
    e_w_out = nrm((ne, E_OUT, D_MODEL), E_OUT ** -0.5)
    n_idx = jnp.arange(p, dtype=f32)
    s5_lambda_re = -0.5 + nrm((ne, g, p), 0.01)
    s5_lambda_im = math.pi * n_idx + nrm((ne, g, p), 0.01)
    s5_log_dt = jax.random.uniform(next(ks), (ne, g), f32, math.log(1e-3), math.log(1e-1))
    s5_b_re = nrm((ne, g, p, gs), (2.0 * gs) ** -0.5)
    s5_b_im = nrm((ne, g, p, gs), (2.0 * gs) ** -0.5)
    s5_c_re = nrm((ne, g, gs, p), (2.0 * p) ** -0.5)
    s5_c_im = nrm((ne, g, gs, p), (2.0 * p) ** -0.5)
    s5_d = nrm((ne, S5_WIDTH), 1.0)
    s5_w_glu = nrm((ne, S5_WIDTH, S5_WIDTH), S5_WIDTH ** -0.5)
    s5_b_glu = nrm((ne, S5_WIDTH), 0.01)
    da_q_norm = 1.0 + nrm((ne, DA_V_DIM), 0.02)
    da_k_norm = 1.0 + nrm((ne, DA_V_DIM), 0.02)
    da_lambda_q1 = nrm((ne, DA_HEAD_DIM), 0.1)
    da_lambda_k1 = nrm((ne, DA_HEAD_DIM), 0.1)
    da_lambda_q2 = nrm((ne, DA_HEAD_DIM), 0.1)
    da_lambda_k2 = nrm((ne, DA_HEAD_DIM), 0.1)
    da_out_norm = 1.0 + nrm((ne, DA_V_DIM), 0.02)
    o_norm = 1.0 + nrm((no, D_MODEL), 0.02)
    o_w_in = nrm((no, D_MODEL, O_IN), D_MODEL ** -0.5)
    o_b_f = jax.random.uniform(next(ks), (no, FOX_HEADS), f32, 1.0, 6.0)
    o_w_out = nrm((no, FOX_WIDTH, D_MODEL), FOX_WIDTH ** -0.5)
    fox_q_norm = 1.0 + nrm((no, FOX_HEAD_DIM), 0.02)
    fox_k_norm = 1.0 + nrm((no, FOX_HEAD_DIM), 0.02)
    return {
        "x": x, "e_norm": e_norm, "e_w_in": e_w_in, "e_w_out": e_w_out,
        "s5_lambda_re": s5_lambda_re, "s5_lambda_im": s5_lambda_im, "s5_log_dt": s5_log_dt,
        "s5_b_re": s5_b_re, "s5_b_im": s5_b_im, "s5_c_re": s5_c_re, "s5_c_im": s5_c_im,
        "s5_d": s5_d, "s5_w_glu": s5_w_glu, "s5_b_glu": s5_b_glu,
        "da_q_norm": da_q_norm, "da_k_norm": da_k_norm,
        "da_lambda_q1": da_lambda_q1, "da_lambda_k1": da_lambda_k1,
        "da_lambda_q2": da_lambda_q2, "da_lambda_k2": da_lambda_k2, "da_out_norm": da_out_norm,
        "o_norm": o_norm, "o_w_in": o_w_in, "o_b_f": o_b_f, "o_w_out": o_w_out,
        "fox_q_norm": fox_q_norm, "fox_k_norm": fox_k_norm,
    }


def reference(x, e_norm, e_w_in, e_w_out, s5_lambda_re, s5_lambda_im, s5_log_dt,
              s5_b_re, s5_b_im, s5_c_re, s5_c_im, s5_d, s5_w_glu, s5_b_glu,
              da_q_norm, da_k_norm, da_lambda_q1, da_lambda_k1, da_lambda_q2, da_lambda_k2,
              da_out_norm, o_norm, o_w_in, o_b_f, o_w_out, fox_q_norm, fox_k_norm):
    e_splits = [S5_WIDTH, 2 * S5_WIDTH, 2 * S5_WIDTH + DA_WIDTH,
                2 * S5_WIDTH + 2 * DA_WIDTH, 2 * S5_WIDTH + 3 * DA_WIDTH]
    o_splits = [FOX_WIDTH, 2 * FOX_WIDTH, 3 * FOX_WIDTH, 4 * FOX_WIDTH]
    for layer in range(DEPTH):
        j = layer // 2
        if layer % 2 == 0:
            h = _rms(x, e_norm[j])
            proj = h @ e_w_in[j]
            u, z_s5, q, k, v, z_da = jnp.split(proj, e_splits, axis=-1)
            y_s5 = _s5_mixer(u, s5_lambda_re[j], s5_lambda_im[j], s5_log_dt[j],
                             s5_b_re[j], s5_b_im[j], s5_c_re[j], s5_c_im[j],
                             s5_d[j], s5_w_glu[j], s5_b_glu[j]) * jax.nn.silu(z_s5)
            lambda_init = 0.8 - 0.6 * math.exp(-0.3 * layer)
            y_da = _diff_attention(q, k, v, da_q_norm[j], da_k_norm[j],
                                   da_lambda_q1[j], da_lambda_k1[j],
                                   da_lambda_q2[j], da_lambda_k2[j],
                                   da_out_norm[j], lambda_init) * jax.nn.silu(z_da)
            x = x + jnp.concatenate([y_s5, y_da], axis=-1) @ e_w_out[j]
        else:
            h = _rms(x, o_norm[j])
            proj = h @ o_w_in[j]
            q, k, v, z, f_logit = jnp.split(proj, o_splits, axis=-1)
            y = _forgetting_attention(q, k, v, f_logit, o_b_f[j],
                                      fox_q_norm[j], fox_k_norm[j]) * jax.nn.silu(z)
            x = x + y @ o_w_out[j]
    return x
```

```python
import functools
import math

import jax
import jax.numpy as jnp
from jax import lax
from jax.experimental import pallas as pl
from jax.experimental.pallas import tpu as pltpu

F32 = jnp.float32
BF16 = jnp.bfloat16
EPS = 1e-6
NEG = -1e30

LANES = 128
MXU_EDGE = 256
SEC = 1024
VMEM_LIMIT = 52 * 1024 * 1024

S5_GROUPS = 64
S5_GROUP = 16
S5_STATE = 64
S5_SLICES = 8
DA_HEADS = 8
DA_HEAD_DIM = 64
FOX_HEADS = 16
HEAD_W = 128


def _params(sem):
    return pltpu.CompilerParams(dimension_semantics=sem, vmem_limit_bytes=VMEM_LIMIT)


def _head_norm_t(acc_t, gain_ref, head_dim):
    tm = acc_t.shape[1]
    r = acc_t.reshape(SEC // head_dim, head_dim, tm)
    ms = jnp.mean(r * r, axis=1, keepdims=True)
    n = (r * lax.rsqrt(ms + EPS)).reshape(SEC, tm)
    g = gain_ref[...]
    return jnp.concatenate(
        [n[:, c * LANES:(c + 1) * LANES] * g for c in range(tm // LANES)], axis=1)


def _proj_kernel(*refs, layout, head_dim, has_logf):
    it = iter(refs)
    x_ref, g_ref, w_ref = next(it), next(it), next(it)
    if has_logf:
        wf_ref, bf_ref = next(it), next(it)
    qg_ref, kg_ref = next(it), next(it)
    out_refs = [next(it) for _ in layout]
    if has_logf:
        logf_ref = next(it)
    hs_ref = next(it)

    j = pl.program_id(1)

    @pl.when(j == 0)
    def _():
        x = x_ref[...]
        ms = jnp.mean(x * x, axis=-1, keepdims=True)
        hs_ref[...] = (x * lax.rsqrt(ms + EPS) * g_ref[...]).astype(BF16)
        if has_logf:
            f = jnp.dot(hs_ref[...], wf_ref[...], preferred_element_type=F32) + bf_ref[...]
            logf_ref[...] = jax.nn.log_sigmoid(f)

    acc = jnp.dot(hs_ref[...], w_ref[...], preferred_element_type=F32)

    for (kind, j0, cnt), o_ref in zip(layout, out_refs):
        @pl.when((j >= j0) & (j < j0 + cnt))
        def _(kind=kind, o_ref=o_ref):
            if kind == "f32":
                o_ref[...] = acc
            elif kind == "vT":
                o_ref[...] = acc.T.astype(BF16)
            elif kind == "qT":
                o_ref[...] = _head_norm_t(acc.T, qg_ref, head_dim).astype(BF16)
            else:
                o_ref[...] = _head_norm_t(acc.T, kg_ref, head_dim).T.astype(BF16)


def _project(x2d, norm_g, w, qgain, kgain, *, batch, seq, tm, layout, head_dim, wf=None, bf=None):
    n_rows, d = x2d.shape
    nsec = w.shape[1] // SEC
    nt = seq // tm
    has_logf = wf is not None

    def sec(j, j0, cnt):
        return jnp.clip(j - j0, 0, cnt - 1)

    in_specs = [
        pl.BlockSpec((tm, d), lambda i, j: (i, 0)),
        pl.BlockSpec((1, d), lambda i, j: (0, 0)),
        pl.BlockSpec((d, SEC), lambda i, j: (0, j)),
    ]
    args = [x2d, norm_g.reshape(1, d), w]
    if has_logf:
        in_specs += [pl.BlockSpec((d, LANES), lambda i, j: (0, 0)),
                     pl.BlockSpec((1, LANES), lambda i, j: (0, 0))]
        args += [wf, bf]
    in_specs += [pl.BlockSpec((SEC, LANES), lambda i, j: (0, 0)),
                 pl.BlockSpec((SEC, LANES), lambda i, j: (0, 0))]
    args += [qgain, kgain]

    out_shapes, out_specs = [], []
    for kind, j0, cnt in layout:
        if kind == "f32" or kind == "k":
            dt = F32 if kind == "f32" else BF16
            out_shapes.append(jax.ShapeDtypeStruct((n_rows, cnt * SEC), dt))
            out_specs.append(pl.BlockSpec(
                (tm, SEC), lambda i, j, j0=j0, cnt=cnt: (i, sec(j, j0, cnt))))
        elif kind == "qT":
            out_shapes.append(jax.ShapeDtypeStruct((batch, cnt * SEC, seq), BF16))
            out_specs.append(pl.BlockSpec(
                (None, SEC, tm), lambda i, j, j0=j0, cnt=cnt: (i // nt, sec(j, j0, cnt), i % nt)))
        else:
            out_shapes.append(jax.ShapeDtypeStruct((batch, nt, cnt * SEC, tm), BF16))
            out_specs.append(pl.BlockSpec(
                (None, None, SEC, tm),
                lambda i, j, j0=j0, cnt=cnt: (i // nt, i % nt, sec(j, j0, cnt), 0)))
    if has_logf:
        out_shapes.append(jax.ShapeDtypeStruct((n_rows, LANES), F32))
        out_specs.append(pl.BlockSpec((tm, LANES), lambda i, j: (i, 0)))

    return pl.pallas_call(
        functools.partial(_proj_kernel, layout=layout, head_dim=head_dim, has_logf=has_logf),
        grid=(n_rows // tm, nsec),
        in_specs=in_specs,
        out_specs=out_specs,
        out_shape=out_shapes,
        scratch_shapes=[pltpu.VMEM((tm, d), BF16)],
        compiler_params=_params(("arbitrary", "arbitrary")),
        name="rms_in_proj",
    )(*args)


def _out_kernel(*refs, n_in):
    ys, ws = refs[:n_in], refs[n_in:2 * n_in]
    x_ref, o_ref = refs[2 * n_in], refs[2 * n_in + 1]
    acc = x_ref[...]
    for y_ref, w_ref in zip(ys, ws):
        acc = acc + jnp.dot(y_ref[...], w_ref[...], preferred_element_type=F32)
    o_ref[...] = acc


def _out_project(ys, ws, x2d, *, tm):
    n_rows, d = x2d.shape
    n_in = len(ys)
    in_specs = [pl.BlockSpec((tm, y.shape[1]), lambda i: (i, 0)) for y in ys]
    in_specs += [pl.BlockSpec(w.shape, lambda i: (0, 0)) for w in ws]
    in_specs += [pl.BlockSpec((tm, d), lambda i: (i, 0))]
    return pl.pallas_call(
        functools.partial(_out_kernel, n_in=n_in),
        grid=(n_rows // tm,),
        in_specs=in_specs,
        out_specs=pl.BlockSpec((tm, d), lambda i: (i, 0)),
        out_shape=jax.ShapeDtypeStruct((n_rows, d), F32),
        compiler_params=_params(("arbitrary",)),
        name="out_proj_residual",
    )(*ys, *ws, x2d)


def _s5_prepare(lam_re, lam_im, log_dt, b_re, b_im, c_re, c_im):
    dt = jnp.exp(log_dt.astype(F32))[:, None]
    lr, li = lam_re.astype(F32), lam_im.astype(F32)
    mag = jnp.exp(lr * dt)
    ar, ai = mag * jnp.cos(li * dt), mag * jnp.sin(li * dt)
    den = lr * lr + li * li
    nr, ni = ar - 1.0, ai
    kr = (nr * lr + ni * li) / den
    ki = (ni * lr - nr * li) / den
    br, bi = b_re.astype(F32), b_im.astype(F32)
    bbr = kr[..., None] * br - ki[..., None] * bi
    bbi = kr[..., None] * bi + ki[..., None] * br
    nblk = S5_GROUPS // 16
    eye = jnp.eye(16, dtype=F32)

    def b_tiles(bb):
        t = bb.reshape(nblk, 16, S5_SLICES, 8, S5_GROUP)
        return jnp.einsum("ngkph,fg->knfhgp", t, eye).reshape(S5_SLICES, nblk, 256, LANES)

    def c_tiles(cc):
        t = cc.reshape(nblk, 16, S5_GROUP, S5_SLICES, 8)
        return jnp.einsum("nghkp,fg->nkgpfh", t, eye).reshape(nblk, S5_SLICES, LANES, 256)

    bmat = jnp.concatenate([b_tiles(bbr), b_tiles(bbi)], axis=-1).astype(BF16)
    cmat = jnp.concatenate([c_tiles(c_re.astype(F32)), c_tiles(-c_im.astype(F32))], axis=2)
    cmat = cmat.reshape(nblk, S5_SLICES * 256, 256).astype(BF16)

    def a_tiles(a):
        t = a.reshape(nblk, 16, S5_SLICES, 8).transpose(2, 0, 1, 3)
        return t.reshape(S5_SLICES, nblk * LANES)

    return bmat, cmat, a_tiles(ar), a_tiles(ai)


def _s5_kernel(u_ref, z_ref, bm_ref, cm_ref, ar_ref, ai_ref, d_ref, wg_ref, bg_ref,
               o_ref, s_ref, h_ref, *, chunk, pitch):
    nb = u_ref.shape[0]
    nblk = ar_ref.shape[1] // LANES

    @pl.when(pl.program_id(0) == 0)
    def _():
        h_ref[...] = jnp.zeros_like(h_ref)

    for b in range(nb):
        ub = u_ref[b].astype(BF16)
        for k in range(S5_SLICES):
            for n in range(nblk):
                x = jnp.dot(ub[:, n * 256:(n + 1) * 256], bm_ref[k, n],
                            preferred_element_type=F32)
                s_ref[b, 2 * n, k * pitch:k * pitch + chunk, :] = x[:, :LANES]
                s_ref[b, 2 * n + 1, k * pitch:k * pitch + chunk, :] = x[:, LANES:]

    ar = [ar_ref[:, n * LANES:(n + 1) * LANES] for n in range(nblk)]
    ai = [ai_ref[:, n * LANES:(n + 1) * LANES] for n in range(nblk)]

    def step(t, hs):
        out = []
        for b in range(nb):
            for n in range(nblk):
                hr, hi = hs[2 * (b * nblk + n)], hs[2 * (b * nblk + n) + 1]
                rows = pl.ds(t, S5_SLICES, stride=pitch)
                xr = s_ref[b, 2 * n, rows, :]
                xi = s_ref[b, 2 * n + 1, rows, :]
                nr = ar[n] * hr - ai[n] * hi + xr
                ni = ar[n] * hi + ai[n] * hr + xi
                s_ref[b, 2 * n, rows, :] = nr
                s_ref[b, 2 * n + 1, rows, :] = ni
                out += [nr, ni]
        return tuple(out)

    h0 = tuple(h_ref[b, s] for b in range(nb) for s in range(2 * nblk))
    hs = lax.fori_loop(0, chunk, step, h0, unroll=4)
    for b in range(nb):
        for s in range(2 * nblk):
            h_ref[b, s] = hs[b * 2 * nblk + s]

    for b in range(nb):
        ys = []
        for n in range(nblk):
            acc = None
            for k in range(S5_SLICES):
                rows = slice(k * pitch, k * pitch + chunk)
                hk = jnp.concatenate([s_ref[b, 2 * n, rows, :], s_ref[b, 2 * n + 1, rows, :]],
                                     axis=1).astype(BF16)
                part = jnp.dot(hk, cm_ref[n, k * 256:(k + 1) * 256, :],
                               preferred_element_type=F32)
                acc = part if acc is None else acc + part
            ys.append(acc)
        y = jnp.concatenate(ys, axis=1) + d_ref[...] * u_ref[b]
        y = jax.nn.gelu(y)
        gate = jax.nn.sigmoid(
            jnp.dot(y.astype(BF16), wg_ref[...], preferred_element_type=F32) + bg_ref[...])
        o_ref[b] = (y * gate * jax.nn.silu(z_ref[b])).astype(BF16)


def _s5_mixer(u, z, bmat, cmat, a_re, a_im, d, w_glu, b_glu, *, chunk):
    batch, seq, width = u.shape
    pitch = chunk + 8
    nblk = width // 256
    const = lambda shape: pl.BlockSpec(shape, lambda c: (0,) * len(shape))
    return pl.pallas_call(
        functools.partial(_s5_kernel, chunk=chunk, pitch=pitch),
        grid=(seq // chunk,),
        in_specs=[
            pl.BlockSpec((batch, chunk, width), lambda c: (0, c, 0)),
            pl.BlockSpec((batch, chunk, width), lambda c: (0, c, 0)),
            const(bmat.shape), const(cmat.shape), const(a_re.shape), const(a_im.shape),
            const((1, width)), const(w_glu.shape), const((1, width)),
        ],
        out_specs=pl.BlockSpec((batch, chunk, width), lambda c: (0, c, 0)),
        out_shape=jax.ShapeDtypeStruct((batch, seq, width), BF16),
        scratch_shapes=[
            pltpu.VMEM((batch, 2 * nblk, S5_SLICES * pitch, LANES), F32),
            pltpu.VMEM((batch, 2 * nblk, S5_SLICES, LANES), F32),
        ],
        compiler_params=_params(("arbitrary",)),
        name="s5_mixer",
    )(u, z, bmat, cmat, a_re, a_im, d.reshape(1, width), w_glu, b_glu.reshape(1, width))


def _tile_lanes(col, n):
    return jnp.concatenate([col] * n, axis=1) if n > 1 else col


def _flash_update(t, vt_blk, m_ref, l_ref, acc_ref):
    m_prev = m_ref[...]
    m_new = jnp.maximum(m_prev, jnp.max(t, axis=0, keepdims=True))
    alpha = jnp.exp(m_prev - m_new)
    p = jnp.exp(t - m_new)
    l_ref[...] = alpha * l_ref[...] + jnp.sum(p, axis=0, keepdims=True)
    pv = jnp.dot(vt_blk, p.astype(BF16), preferred_element_type=F32)
    acc_ref[...] = alpha * acc_ref[...] + pv
    m_ref[...] = m_new


def _causal_t(t):
    key = lax.broadcasted_iota(jnp.int32, t.shape, 0)
    qry = lax.broadcasted_iota(jnp.int32, t.shape, 1)
    return jnp.where(key <= qry, t, NEG)


def _fox_kernel(qt_ref, k_ref, vt_ref, c_ref, z_ref, o_ref, cb_ref, m_ref, l_ref, acc_ref,
                *, tile, scale):
    qi = pl.program_id(2)
    nlt = tile // LANES

    @pl.when(qi == 0)
    def _():
        def fill(r, carry):
            row = c_ref[pl.ds(r, 1), :]
            blk = jnp.broadcast_to(row, (LANES, LANES)).T
            cb_ref[pl.ds(pl.multiple_of(r * LANES, LANES), LANES), :] = blk
            return carry
        lax.fori_loop(0, c_ref.shape[0], fill, 0)

    qt = qt_ref[...]
    m_ref[...] = jnp.full_like(m_ref, NEG)
    l_ref[...] = jnp.zeros_like(l_ref)
    acc_ref[...] = jnp.zeros_like(acc_ref)
    q0 = pl.multiple_of(qi * tile, tile)
    c0 = cb_ref[pl.ds(q0, 1), :]

    def scores(kb):
        ks = pl.multiple_of(kb * tile, tile)
        st = jnp.dot(k_ref[pl.ds(ks, tile), :], qt, preferred_element_type=F32)
        bias = c0 - cb_ref[pl.ds(ks, tile), :]
        return st * scale + _tile_lanes(bias, nlt)

    def body(kb, carry):
        _flash_update(scores(kb), vt_ref[kb], m_ref, l_ref, acc_ref)
        return carry

    lax.fori_loop(0, qi, body, 0)
    _flash_update(_causal_t(scores(qi)), vt_ref[qi], m_ref, l_ref, acc_ref)

    o = (acc_ref[...] * (1.0 / l_ref[...])).T
    o_ref[...] = (o * jax.nn.silu(z_ref[...])).astype(BF16)


def _fox_attention(qt, k, vt, c2, z, *, tile):
    batch, width, seq = qt.shape
    heads = width // HEAD_W
    nq = seq // tile
    scale = HEAD_W ** -0.5
    return pl.pallas_call(
        functools.partial(_fox_kernel, tile=tile, scale=scale),
        grid=(batch, heads, nq),
        in_specs=[
            pl.BlockSpec((None, HEAD_W, tile), lambda b, h, q: (b, h, q)),
            pl.BlockSpec((None, seq, HEAD_W), lambda b, h, q: (b, 0, h)),
            pl.BlockSpec((None, nq, HEAD_W, tile), lambda b, h, q: (b, 0, h, 0)),
            pl.BlockSpec((None, None, seq // LANES, LANES), lambda b, h, q: (b, h, 0, 0)),
            pl.BlockSpec((None, tile, HEAD_W), lambda b, h, q: (b, q, h)),
        ],
        out_specs=pl.BlockSpec((None, tile, HEAD_W), lambda b, h, q: (b, q, h)),
        out_shape=jax.ShapeDtypeStruct((batch, seq, width), BF16),
        scratch_shapes=[
            pltpu.VMEM((seq, LANES), F32),
            pltpu.VMEM((1, tile), F32), pltpu.VMEM((1, tile), F32),
            pltpu.VMEM((HEAD_W, tile), F32),
        ],
        compiler_params=_params(("arbitrary", "arbitrary", "arbitrary")),
        name="fox_attention",
    )(qt, k, vt, c2, z)


def _diff_kernel(qt_ref, k_ref, vt_ref, hp_ref, og_ref, z_ref, o_ref,
                 m0_ref, l0_ref, a0_ref, m1_ref, l1_ref, a1_ref, *, tile, out_scale):
    qi = pl.program_id(2)
    nlt = tile // LANES
    dh = qt_ref.shape[0] // 2

    qt = qt_ref[...]
    half = lax.broadcasted_iota(jnp.int32, qt.shape, 0) < dh
    zero = jnp.zeros_like(qt)
    qt0 = jnp.where(half, qt, zero)
    qt1 = jnp.where(half, zero, qt)
    for m_ref, l_ref, a_ref in ((m0_ref, l0_ref, a0_ref), (m1_ref, l1_ref, a1_ref)):
        m_ref[...] = jnp.full_like(m_ref, NEG)
        l_ref[...] = jnp.zeros_like(l_ref)
        a_ref[...] = jnp.zeros_like(a_ref)
    slope = hp_ref[0:1, :]
    lam = hp_ref[1:2, 0:1]
    q0 = qi * tile

    def scores(kb):
        ks = pl.multiple_of(kb * tile, tile)
        kblk = k_ref[pl.ds(ks, tile), :]
        rel = (lax.broadcasted_iota(jnp.int32, (tile, LANES), 0) + (ks - q0)).astype(F32)
        bias = _tile_lanes(slope * rel, nlt)
        s0 = jnp.dot(kblk, qt0, preferred_element_type=F32) + bias
        s1 = jnp.dot(kblk, qt1, preferred_element_type=F32) + bias
        return s0, s1

    def body(kb, carry):
        s0, s1 = scores(kb)
        vt = vt_ref[kb]
        _flash_update(s0, vt, m0_ref, l0_ref, a0_ref)
        _flash_update(s1, vt, m1_ref, l1_ref, a1_ref)
        return carry

    lax.fori_loop(0, qi, body, 0)
    s0, s1 = scores(qi)
    vt = vt_ref[qi]
    _flash_update(_causal_t(s0), vt, m0_ref, l0_ref, a0_ref)
    _flash_update(_causal_t(s1), vt, m1_ref, l1_ref, a1_ref)

    ot = a0_ref[...] * (1.0 / l0_ref[...]) - lam * (a1_ref[...] * (1.0 / l1_ref[...]))
    ms = jnp.mean(ot * ot, axis=0, keepdims=True)
    o = (ot * lax.rsqrt(ms + EPS)).T
    o = o * og_ref[...] * out_scale
    o_ref[...] = (o * jax.nn.silu(z_ref[...])).astype(BF16)


def _diff_attention(qt, k, vt, head_params, out_gain, z, *, tile, out_scale):
    batch, width, seq = qt.shape
    heads = width // HEAD_W
    nq = seq // tile
    stat = lambda: pltpu.VMEM((1, tile), F32)
    accum = lambda: pltpu.VMEM((HEAD_W, tile), F32)
    return pl.pallas_call(
        functools.partial(_diff_kernel, tile=tile, out_scale=out_scale),
        grid=(batch, heads, nq),
        in_specs=[
            pl.BlockSpec((None, HEAD_W, tile), lambda b, h, q: (b, h, q)),
            pl.BlockSpec((None, seq, HEAD_W), lambda b, h, q: (b, 0, h)),
            pl.BlockSpec((None, nq, HEAD_W, tile), lambda b, h, q: (b, 0, h, 0)),
            pl.BlockSpec((None, 8, LANES), lambda b, h, q: (h, 0, 0)),
            pl.BlockSpec((1, HEAD_W), lambda b, h, q: (0, 0)),
            pl.BlockSpec((None, tile, HEAD_W), lambda b, h, q: (b, q, h)),
        ],
        out_specs=pl.BlockSpec((None, tile, HEAD_W), lambda b, h, q: (b, q, h)),
        out_shape=jax.ShapeDtypeStruct((batch, seq, width), BF16),
        scratch_shapes=[stat(), stat(), accum(), stat(), stat(), accum()],
        compiler_params=_params(("arbitrary", "arbitrary", "arbitrary")),
        name="diff_attention",
    )(qt, k, vt, head_params, out_gain, z)


def _cumsum_kernel(lf_ref, o_ref, carry_ref, *, rows):
    @pl.when(pl.program_id(1) == 0)
    def _():
        carry_ref[...] = jnp.zeros_like(carry_ref)

    x = lf_ref[...]
    r = lax.broadcasted_iota(jnp.int32, (rows, rows), 0)
    c = lax.broadcasted_iota(jnp.int32, (rows, rows), 1)
    tri = (c <= r).astype(BF16)
    hi = x.astype(BF16)
    r1 = x - hi.astype(F32)
    mid = r1.astype(BF16)
    lo = (r1 - mid.astype(F32)).astype(BF16)
    cum = (jnp.dot(tri, hi, preferred_element_type=F32)
           + jnp.dot(tri, mid, preferred_element_type=F32)
           + jnp.dot(tri, lo, preferred_element_type=F32)) + carry_ref[...]
    carry_ref[...] = cum[rows - 1:rows, :]
    o_ref[...] = cum.T


def _cumsum_t(logf, *, rows):
    batch, seq, _ = logf.shape
    return pl.pallas_call(
        functools.partial(_cumsum_kernel, rows=rows),
        grid=(batch, seq // rows),
        in_specs=[pl.BlockSpec((None, rows, LANES), lambda b, i: (b, i, 0))],
        out_specs=pl.BlockSpec((None, LANES, rows), lambda b, i: (b, 0, i)),
        out_shape=jax.ShapeDtypeStruct((batch, LANES, seq), F32),
        scratch_shapes=[pltpu.VMEM((1, LANES), F32)],
        compiler_params=_params(("arbitrary", "arbitrary")),
        name="logf_cumsum",
    )(logf)


def _lane_col(v):
    return jnp.broadcast_to(v.astype(F32)[:, None], (v.shape[0], LANES))


def _alibi_slopes(n):
    return jnp.asarray([2.0 ** (-8.0 * (h + 1) / n) for h in range(n)], F32)


def _even_layer(x2d, layer, batch, seq, tiles, e_norm, e_w_in, e_w_out, lam_re, lam_im, log_dt,
                b_re, b_im, c_re, c_im, d, w_glu, b_glu, q_norm, k_norm, lq1, lk1, lq2, lk2,
                out_norm):
    tm, tile, chunk = tiles
    s5w = S5_GROUPS * S5_GROUP
    lambda_init = 0.8 - 0.6 * math.exp(-0.3 * layer)
    qgain = _lane_col(jnp.tile(q_norm.astype(F32) * (DA_HEAD_DIM ** -0.5), DA_HEADS))
    kgain = _lane_col(jnp.tile(k_norm, DA_HEADS))
    layout = (("f32", 0, 1), ("f32", 1, 1), ("qT", 2, 1), ("k", 3, 1), ("vT", 4, 1), ("f32", 5, 1))
    u, zs, qt, k, vt, zd = _project(
        x2d, e_norm, e_w_in.astype(BF16), qgain, kgain, batch=batch, seq=seq, tm=tm,
        layout=layout, head_dim=DA_HEAD_DIM)

    bmat, cmat, a_re, a_im = _s5_prepare(lam_re, lam_im, log_dt, b_re, b_im, c_re, c_im)
    y_s5 = _s5_mixer(u.reshape(batch, seq, s5w), zs.reshape(batch, seq, s5w), bmat, cmat,
                     a_re, a_im, d, w_glu.astype(BF16), b_glu, chunk=chunk)

    f32 = F32
    lam = (jnp.exp(jnp.sum(lq1.astype(f32) * lk1.astype(f32)))
           - jnp.exp(jnp.sum(lq2.astype(f32) * lk2.astype(f32))) + lambda_init)
    head_params = jnp.zeros((DA_HEADS, 8, LANES), F32)
    head_params = head_params.at[:, 0, :].set(_alibi_slopes(DA_HEADS)[:, None])
    head_params = head_params.at[:, 1, :].set(lam)
    y_da = _diff_attention(qt, k.reshape(batch, seq, -1), vt, head_params,
                           out_norm.astype(F32).reshape(1, HEAD_W),
                           zd.reshape(batch, seq, -1), tile=tile, out_scale=1.0 - lambda_init)

    w_out = e_w_out.astype(BF16)
    return _out_project([y_s5.reshape(batch * seq, s5w), y_da.reshape(batch * seq, -1)],
                        [w_out[:s5w], w_out[s5w:]], x2d, tm=tm)


def _odd_layer(x2d, batch, seq, tiles, o_norm, o_w_in, o_b_f, o_w_out, q_norm, k_norm):
    tm, tile, _ = tiles
    width = FOX_HEADS * HEAD_W
    nsec = width // SEC
    qgain = _lane_col(jnp.tile(q_norm, SEC // HEAD_W))
    kgain = _lane_col(jnp.tile(k_norm, SEC // HEAD_W))
    w = o_w_in.astype(BF16)
    wf = jnp.zeros((w.shape[0], LANES), BF16).at[:, :FOX_HEADS].set(w[:, 4 * width:])
    bf = jnp.zeros((1, LANES), F32).at[0, :FOX_HEADS].set(o_b_f.astype(F32))
    layout = (("qT", 0, nsec), ("k", nsec, nsec), ("vT", 2 * nsec, nsec), ("f32", 3 * nsec, nsec))
    qt, k, vt, z, logf = _project(
        x2d, o_norm, w[:, :4 * width], qgain, kgain, batch=batch, seq=seq, tm=tm,
        layout=layout, head_dim=HEAD_W, wf=wf, bf=bf)
    ct = _cumsum_t(logf.reshape(batch, seq, LANES), rows=tm)
    c2 = ct[:, :FOX_HEADS].reshape(batch, FOX_HEADS, seq // LANES, LANES)
    y = _fox_attention(qt, k.reshape(batch, seq, width), vt, c2, z.reshape(batch, seq, width),
                       tile=tile)
    return _out_project([y.reshape(batch * seq, width)], [o_w_out.astype(BF16)], x2d, tm=tm)


def _tiles(seq):
    tm = min(512, seq)
    return tm, tm, min(128, seq)


def kernel(x, e_norm, e_w_in, e_w_out, s5_lambda_re, s5_lambda_im, s5_log_dt, s5_b_re, s5_b_im, s5_c_re, s5_c_im, s5_d, s5_w_glu, s5_b_glu, da_q_norm, da_k_norm, da_lambda_q1, da_lambda_k1, da_lambda_q2, da_lambda_k2, da_out_norm, o_norm, o_w_in, o_b_f, o_w_out, fox_q_norm, fox_k_norm):
    batch, seq, d_model = x.shape
    tiles = _tiles(seq)
    depth = e_norm.shape[0] + o_norm.shape[0]
    x2d = x.reshape(batch * seq, d_model)
    for layer in range(depth):
        j = layer // 2
        if layer % 2 == 0:
            x2d = _even_layer(
                x2d, layer, batch, seq, tiles, e_norm[j], e_w_in[j], e_w_out[j],
                s5_lambda_re[j], s5_lambda_im[j], s5_log_dt[j], s5_b_re[j], s5_b_im[j],
                s5_c_re[j], s5_c_im[j], s5_d[j], s5_w_glu[j], s5_b_glu[j],
                da_q_norm[j], da_k_norm[j], da_lambda_q1[j], da_lambda_k1[j],
                da_lambda_q2[j], da_lambda_k2[j], da_out_norm[j])
        else:
            x2d = _odd_layer(x2d, batch, seq, tiles, o_norm[j], o_w_in[j], o_b_f[j], o_w_out[j],
                             fox_q_norm[j], fox_k_norm[j])
    return x2d.reshape(batch, seq, d_model)
```

```python
import functools
import math

import jax
import jax.numpy as jnp
from jax import lax
from jax.experimental import pallas as pl
from jax.experimental.pallas import tpu as pltpu

F32 = jnp.float32
BF16 = jnp.bfloat16
EPS = 1e-6
NEG = -1e30
LOG2E = math.log2(math.e)

LANES = 128
BF16_ROWS = 16
AUG = 256
CHUNK = 256
SEC = 1024
VMEM_LIMIT = 52 * 1024 * 1024

S5_GROUPS = 64
S5_GROUP = 16
S5_STATE = 64
S5_SLICES = 8
DA_HEADS = 8
DA_HEAD_DIM = 64
FOX_HEADS = 16
HEAD_W = 128


def _params(sem):
    return pltpu.CompilerParams(dimension_semantics=sem, vmem_limit_bytes=VMEM_LIMIT)


def _head_norm_t(acc_t, gain_ref, head_dim):
    tm = acc_t.shape[1]
    r = acc_t.reshape(SEC // head_dim, head_dim, tm)
    ms = jnp.mean(r * r, axis=1, keepdims=True)
    n = (r * lax.rsqrt(ms + EPS)).reshape(SEC, tm)
    g = gain_ref[...]
    return jnp.concatenate(
        [n[:, c * LANES:(c + 1) * LANES] * g for c in range(tm // LANES)], axis=1)


def _proj_kernel(*refs, layout, head_dim, has_logf):
    it = iter(refs)
    x_ref, g_ref, w_ref = next(it), next(it), next(it)
    if has_logf:
        wf_ref, bf_ref = next(it), next(it)
    qg_ref, kg_ref = next(it), next(it)
    out_refs = [next(it) for _ in layout]
    if has_logf:
        logf_ref = next(it)
    hs_ref = next(it)

    j = pl.program_id(1)

    @pl.when(j == 0)
    def _():
        x = x_ref[...]
        ms = jnp.mean(x * x, axis=-1, keepdims=True)
        hs_ref[...] = (x * lax.rsqrt(ms + EPS) * g_ref[...]).astype(BF16)
        if has_logf:
            f = jnp.dot(hs_ref[...], wf_ref[...], preferred_element_type=F32) + bf_ref[...]
            logf_ref[...] = jax.nn.log_sigmoid(f)

    acc = jnp.dot(hs_ref[...], w_ref[...], preferred_element_type=F32)

    for (kind, j0, cnt), o_ref in zip(layout, out_refs):
        @pl.when((j >= j0) & (j < j0 + cnt))
        def _(kind=kind, o_ref=o_ref):
            if kind == "f32":
                o_ref[...] = acc
            elif kind == "vT":
                o_ref[...] = acc.T.astype(BF16)
            elif kind == "qT":
                o_ref[...] = _head_norm_t(acc.T, qg_ref, head_dim).astype(BF16)
            else:
                o_ref[...] = _head_norm_t(acc.T, kg_ref, head_dim).T.astype(BF16)


def _project(x2d, norm_g, w, qgain, kgain, *, batch, seq, tm, layout, head_dim, wf=None, bf=None):
    n_rows, d = x2d.shape
    nsec = w.shape[1] // SEC
    nt = seq // tm
    has_logf = wf is not None

    def sec(j, j0, cnt):
        return jnp.clip(j - j0, 0, cnt - 1)

    in_specs = [
        pl.BlockSpec((tm, d), lambda i, j: (i, 0)),
        pl.BlockSpec((1, d), lambda i, j: (0, 0)),
        pl.BlockSpec((d, SEC), lambda i, j: (0, j)),
    ]
    args = [x2d, norm_g.reshape(1, d), w]
    if has_logf:
        in_specs += [pl.BlockSpec((d, LANES), lambda i, j: (0, 0)),
                     pl.BlockSpec((1, LANES), lambda i, j: (0, 0))]
        args += [wf, bf]
    in_specs += [pl.BlockSpec((SEC, LANES), lambda i, j: (0, 0)),
                 pl.BlockSpec((SEC, LANES), lambda i, j: (0, 0))]
    args += [qgain, kgain]

    out_shapes, out_specs = [], []
    for kind, j0, cnt in layout:
        if kind == "f32" or kind == "k":
            dt = F32 if kind == "f32" else BF16
            out_shapes.append(jax.ShapeDtypeStruct((n_rows, cnt * SEC), dt))
            out_specs.append(pl.BlockSpec(
                (tm, SEC), lambda i, j, j0=j0, cnt=cnt: (i, sec(j, j0, cnt))))
        elif kind == "qT":
            out_shapes.append(jax.ShapeDtypeStruct((batch, cnt * SEC, seq), BF16))
            out_specs.append(pl.BlockSpec(
                (None, SEC, tm), lambda i, j, j0=j0, cnt=cnt: (i // nt, sec(j, j0, cnt), i % nt)))
        else:
            out_shapes.append(jax.ShapeDtypeStruct((batch, nt, cnt * SEC, tm), BF16))
            out_specs.append(pl.BlockSpec(
                (None, None, SEC, tm),
                lambda i, j, j0=j0, cnt=cnt: (i // nt, i % nt, sec(j, j0, cnt), 0)))
    if has_logf:
        out_shapes.append(jax.ShapeDtypeStruct((n_rows, LANES), F32))
        out_specs.append(pl.BlockSpec((tm, LANES), lambda i, j: (i, 0)))

    return pl.pallas_call(
        functools.partial(_proj_kernel, layout=layout, head_dim=head_dim, has_logf=has_logf),
        grid=(n_rows // tm, nsec),
        in_specs=in_specs,
        out_specs=out_specs,
        out_shape=out_shapes,
        scratch_shapes=[pltpu.VMEM((tm, d), BF16)],
        compiler_params=_params(("arbitrary", "arbitrary")),
        name="rms_in_proj",
    )(*args)


def _out_kernel(*refs, n_in):
    ys, ws = refs[:n_in], refs[n_in:2 * n_in]
    x_ref, o_ref = refs[2 * n_in], refs[2 * n_in + 1]
    acc = x_ref[...]
    for y_ref, w_ref in zip(ys, ws):
        acc = acc + jnp.dot(y_ref[...], w_ref[...], preferred_element_type=F32)
    o_ref[...] = acc


def _out_project(ys, ws, x2d, *, tm):
    n_rows, d = x2d.shape
    n_in = len(ys)
    in_specs = [pl.BlockSpec((tm, y.shape[1]), lambda i: (i, 0)) for y in ys]
    in_specs += [pl.BlockSpec(w.shape, lambda i: (0, 0)) for w in ws]
    in_specs += [pl.BlockSpec((tm, d), lambda i: (i, 0))]
    return pl.pallas_call(
        functools.partial(_out_kernel, n_in=n_in),
        grid=(n_rows // tm,),
        in_specs=in_specs,
        out_specs=pl.BlockSpec((tm, d), lambda i: (i, 0)),
        out_shape=jax.ShapeDtypeStruct((n_rows, d), F32),
        compiler_params=_params(("arbitrary",)),
        name="out_proj_residual",
    )(*ys, *ws, x2d)


def _s5_prepare(lam_re, lam_im, log_dt, b_re, b_im, c_re, c_im):
    dt = jnp.exp(log_dt.astype(F32))[:, None]
    lr, li = lam_re.astype(F32), lam_im.astype(F32)
    mag = jnp.exp(lr * dt)
    ar, ai = mag * jnp.cos(li * dt), mag * jnp.sin(li * dt)
    den = lr * lr + li * li
    nr, ni = ar - 1.0, ai
    kr = (nr * lr + ni * li) / den
    ki = (ni * lr - nr * li) / den
    br, bi = b_re.astype(F32), b_im.astype(F32)
    bbr = kr[..., None] * br - ki[..., None] * bi
    bbi = kr[..., None] * bi + ki[..., None] * br
    nblk = S5_GROUPS // 16
    eye = jnp.eye(16, dtype=F32)

    def b_tiles(bb):
        t = bb.reshape(nblk, 16, S5_SLICES, 8, S5_GROUP)
        return jnp.einsum("ngkph,fg->knfhgp", t, eye).reshape(S5_SLICES, nblk, 256, LANES)

    def c_tiles(cc):
        t = cc.reshape(nblk, 16, S5_GROUP, S5_SLICES, 8)
        return jnp.einsum("nghkp,fg->nkgpfh", t, eye).reshape(nblk, S5_SLICES, LANES, 256)

    bmat = jnp.concatenate([b_tiles(bbr), b_tiles(bbi)], axis=-1).astype(BF16)
    cmat = jnp.concatenate([c_tiles(c_re.astype(F32)), c_tiles(-c_im.astype(F32))], axis=2)
    cmat = cmat.reshape(nblk, S5_SLICES * 256, 256).astype(BF16)

    def a_tiles(a):
        t = a.reshape(nblk, 16, S5_SLICES, 8).transpose(2, 0, 1, 3)
        return t.reshape(S5_SLICES, nblk * LANES)

    return bmat, cmat, a_tiles(ar), a_tiles(ai)


def _s5_kernel(u_ref, z_ref, bm_ref, cm_ref, ar_ref, ai_ref, d_ref, wg_ref, bg_ref,
               o_ref, s_ref, h_ref, *, chunk, pitch):
    nb = u_ref.shape[0]
    nblk = ar_ref.shape[1] // LANES

    @pl.when(pl.program_id(0) == 0)
    def _():
        h_ref[...] = jnp.zeros_like(h_ref)

    for b in range(nb):
        ub = u_ref[b].astype(BF16)
        for k in range(S5_SLICES):
            for n in range(nblk):
                x = jnp.dot(ub[:, n * 256:(n + 1) * 256], bm_ref[k, n],
                            preferred_element_type=F32)
                s_ref[b, 2 * n, k * pitch:k * pitch + chunk, :] = x[:, :LANES]
                s_ref[b, 2 * n + 1, k * pitch:k * pitch + chunk, :] = x[:, LANES:]

    ar = [ar_ref[:, n * LANES:(n + 1) * LANES] for n in range(nblk)]
    ai = [ai_ref[:, n * LANES:(n + 1) * LANES] for n in range(nblk)]

    def step(t, hs):
        out = []
        for b in range(nb):
            for n in range(nblk):
                hr, hi = hs[2 * (b * nblk + n)], hs[2 * (b * nblk + n) + 1]
                rows = pl.ds(t, S5_SLICES, stride=pitch)
                xr = s_ref[b, 2 * n, rows, :]
                xi = s_ref[b, 2 * n + 1, rows, :]
                nr = ar[n] * hr - ai[n] * hi + xr
                ni = ar[n] * hi + ai[n] * hr + xi
                s_ref[b, 2 * n, rows, :] = nr
                s_ref[b, 2 * n + 1, rows, :] = ni
                out += [nr, ni]
        return tuple(out)

    h0 = tuple(h_ref[b, s] for b in range(nb) for s in range(2 * nblk))
    hs = lax.fori_loop(0, chunk, step, h0, unroll=4)
    for b in range(nb):
        for s in range(2 * nblk):
            h_ref[b, s] = hs[b * 2 * nblk + s]

    for b in range(nb):
        ys = []
        for n in range(nblk):
            acc = None
            for k in range(S5_SLICES):
                rows = slice(k * pitch, k * pitch + chunk)
                hk = jnp.concatenate([s_ref[b, 2 * n, rows, :], s_ref[b, 2 * n + 1, rows, :]],
                                     axis=1).astype(BF16)
                part = jnp.dot(hk, cm_ref[n, k * 256:(k + 1) * 256, :],
                               preferred_element_type=F32)
                acc = part if acc is None else acc + part
            ys.append(acc)
        y = jnp.concatenate(ys, axis=1) + d_ref[...] * u_ref[b]
        y = jax.nn.gelu(y)
        gate = jax.nn.sigmoid(
            jnp.dot(y.astype(BF16), wg_ref[...], preferred_element_type=F32) + bg_ref[...])
        o_ref[b] = (y * gate * jax.nn.silu(z_ref[b])).astype(BF16)


def _s5_mixer(u, z, bmat, cmat, a_re, a_im, d, w_glu, b_glu, *, chunk):
    batch, seq, width = u.shape
    pitch = chunk + 8
    nblk = width // 256
    const = lambda shape: pl.BlockSpec(shape, lambda c: (0,) * len(shape))
    return pl.pallas_call(
        functools.partial(_s5_kernel, chunk=chunk, pitch=pitch),
        grid=(seq // chunk,),
        in_specs=[
            pl.BlockSpec((batch, chunk, width), lambda c: (0, c, 0)),
            pl.BlockSpec((batch, chunk, width), lambda c: (0, c, 0)),
            const(bmat.shape), const(cmat.shape), const(a_re.shape), const(a_im.shape),
            const((1, width)), const(w_glu.shape), const((1, width)),
        ],
        out_specs=pl.BlockSpec((batch, chunk, width), lambda c: (0, c, 0)),
        out_shape=jax.ShapeDtypeStruct((batch, seq, width), BF16),
        scratch_shapes=[
            pltpu.VMEM((batch, 2 * nblk, S5_SLICES * pitch, LANES), F32),
            pltpu.VMEM((batch, 2 * nblk, S5_SLICES, LANES), F32),
        ],
        compiler_params=_params(("arbitrary",)),
        name="s5_mixer",
    )(u, z, bmat, cmat, a_re, a_im, d.reshape(1, width), w_glu, b_glu.reshape(1, width))


def _split3(x):
    hi = x.astype(BF16).astype(F32)
    r = x - hi
    mid = r.astype(BF16).astype(F32)
    return hi, mid, r - mid


def _select_rows(index, values):
    out = jnp.zeros(index.shape, F32)
    for i, v in enumerate(values):
        out = jnp.where(index == i, v, out)
    return out


class _Chain:
    def __init__(self, kaug_blk, vt_blk, qaug_ref, stats, c, mult, mask_off):
        self.kaug_blk, self.vt_blk, self.qaug_ref = kaug_blk, vt_blk, qaug_ref
        self.m_ref, self.l_ref, self.acc_ref = stats
        self.lanes = slice(c * CHUNK, (c + 1) * CHUNK)
        self.mult, self.mask_off = mult, mask_off

    def scores(self):
        s = jnp.dot(self.kaug_blk, self.qaug_ref[:, self.lanes],
                    preferred_element_type=F32)
        if self.mask_off is not None:
            key = lax.broadcasted_iota(jnp.int32, s.shape, 0)
            qry = lax.broadcasted_iota(jnp.int32, s.shape, 1)
            s = jnp.where(key - qry <= self.mask_off, s, NEG)
        return s * self.mult

    def softmax(self, t):
        m_prev = self.m_ref[:, self.lanes]
        m_new = jnp.maximum(m_prev, jnp.max(t, axis=0, keepdims=True))
        alpha = jnp.exp2(m_prev - m_new)
        p = jnp.exp2(t - m_new)
        self.l_ref[:, self.lanes] = (alpha * self.l_ref[:, self.lanes]
                                     + jnp.sum(p, axis=0, keepdims=True))
        self.m_ref[:, self.lanes] = m_new
        return p.astype(BF16), alpha

    def values(self, p, alpha):
        pv = jnp.dot(self.vt_blk, p, preferred_element_type=F32)
        self.acc_ref[:, self.lanes] = alpha * self.acc_ref[:, self.lanes] + pv


def _run_chains(chains):
    if not chains:
        return
    nxt = chains[0].scores()
    for i, ch in enumerate(chains):
        t = nxt
        if i + 1 < len(chains):
            nxt = chains[i + 1].scores()
        p, alpha = ch.softmax(t)
        ch.values(p, alpha)


def _causal_sweep(qi, tq, tk, load, make):
    per, nch = tq // tk, tq // CHUNK

    def body(j, carry):
        chains = []
        for d in range(per):
            blk = load(j * per + d)
            for c in range(nch):
                chains += make(blk, c, None)
        _run_chains(chains)
        return carry

    lax.fori_loop(0, qi, body, 0)
    chains = []
    for d in range(per):
        blk = load(qi * per + d)
        for c in range(nch):
            off = c * CHUNK - d * tk
            if off < -(CHUNK - 1):
                continue
            chains += make(blk, c, None if off >= tk - 1 else off)
    _run_chains(chains)


def _init_stats(*triples):
    for m_ref, l_ref, a_ref in triples:
        m_ref[...] = jnp.full_like(m_ref, NEG)
        l_ref[...] = jnp.zeros_like(l_ref)
        a_ref[...] = jnp.zeros_like(a_ref)


def _fox_kernel(qt_ref, k_ref, vt_ref, c_ref, z_ref, o_ref,
                kaug_ref, qaug_ref, m_ref, l_ref, acc_ref, *, tq, tk, scale):
    qi = pl.program_id(2)
    inv = 1.0 / scale

    @pl.when(qi == 0)
    def _():
        lane = lax.broadcasted_iota(jnp.int32, (LANES, LANES), 1)

        def fill(r, carry):
            rows = pl.ds(pl.multiple_of(r * LANES, LANES), LANES)
            col = jnp.broadcast_to(c_ref[pl.ds(r, 1), :], (LANES, LANES)).T
            hi, mid, lo = _split3(col * (-inv))
            ext = _select_rows(lane, (hi, mid, lo, 1.0, 1.0, 1.0))
            kaug_ref[rows, :LANES] = k_ref[rows, :]
            kaug_ref[rows, LANES:] = ext.astype(BF16)
            return carry

        lax.fori_loop(0, c_ref.shape[0], fill, 0)

    crow = c_ref[pl.ds(qi * (tq // LANES), 1), :]
    c0 = jnp.broadcast_to(crow[:, 0:1], (BF16_ROWS, tq)) * inv
    hi, mid, lo = _split3(c0)
    row = lax.broadcasted_iota(jnp.int32, (BF16_ROWS, tq), 0)
    qaug_ref[:HEAD_W, :] = qt_ref[...]
    qaug_ref[HEAD_W:HEAD_W + BF16_ROWS, :] = _select_rows(row, (1.0, 1.0, 1.0, hi, mid, lo)).astype(BF16)
    qaug_ref[HEAD_W + BF16_ROWS:, :] = jnp.zeros((AUG - HEAD_W - BF16_ROWS, tq), BF16)
    _init_stats((m_ref, l_ref, acc_ref))

    def load(kb):
        ks = pl.multiple_of(kb * tk, tk)
        return kaug_ref[pl.ds(ks, tk), :], vt_ref[kb]

    def make(blk, c, mask_off):
        return [_Chain(blk[0], blk[1], qaug_ref, (m_ref, l_ref, acc_ref), c, scale * LOG2E,
                       mask_off)]

    _causal_sweep(qi, tq, tk, load, make)

    o = (acc_ref[...] * (1.0 / l_ref[...])).T
    o_ref[...] = (o * jax.nn.silu(z_ref[...])).astype(BF16)


def _fox_attention(qt, k, vt, c2, z, *, tq, tk):
    batch, width, seq = qt.shape
    heads = width // HEAD_W
    scale = HEAD_W ** -0.5
    return pl.pallas_call(
        functools.partial(_fox_kernel, tq=tq, tk=tk, scale=scale),
        grid=(batch, heads, seq // tq),
        in_specs=[
            pl.BlockSpec((None, HEAD_W, tq), lambda b, h, q: (b, h, q)),
            pl.BlockSpec((None, seq, HEAD_W), lambda b, h, q: (b, 0, h)),
            pl.BlockSpec((None, seq // tk, HEAD_W, tk), lambda b, h, q: (b, 0, h, 0)),
            pl.BlockSpec((None, None, seq // LANES, LANES), lambda b, h, q: (b, h, 0, 0)),
            pl.BlockSpec((None, tq, HEAD_W), lambda b, h, q: (b, q, h)),
        ],
        out_specs=pl.BlockSpec((None, tq, HEAD_W), lambda b, h, q: (b, q, h)),
        out_shape=jax.ShapeDtypeStruct((batch, seq, width), BF16),
        scratch_shapes=[
            pltpu.VMEM((seq, AUG), BF16), pltpu.VMEM((AUG, tq), BF16),
            pltpu.VMEM((1, tq), F32), pltpu.VMEM((1, tq), F32), pltpu.VMEM((HEAD_W, tq), F32),
        ],
        compiler_params=_params(("arbitrary", "arbitrary", "arbitrary")),
        name="fox_attention",
    )(qt, k, vt, c2, z)


def _diff_kernel(qt_ref, k_ref, vt_ref, hp_ref, og_ref, z_ref, o_ref,
                 kaug_ref, qa0_ref, qa1_ref, m0_ref, l0_ref, a0_ref, m1_ref, l1_ref, a1_ref,
                 *, tq, tk, out_scale):
    qi = pl.program_id(2)
    dh = qt_ref.shape[0] // 2
    slope = hp_ref[0:1, :]
    lam = hp_ref[1:2, 0:1]

    @pl.when(qi == 0)
    def _():
        lane = lax.broadcasted_iota(jnp.int32, (LANES, LANES), 1)
        key = lax.broadcasted_iota(jnp.int32, (LANES, LANES), 0)

        def fill(r, carry):
            rows = pl.ds(pl.multiple_of(r * LANES, LANES), LANES)
            base = (key * 0 + r * LANES).astype(F32)
            ext = _select_rows(lane, (slope * base, slope * key.astype(F32), 1.0))
            kaug_ref[rows, :LANES] = k_ref[rows, :]
            kaug_ref[rows, LANES:] = ext.astype(BF16)
            return carry

        lax.fori_loop(0, k_ref.shape[0] // LANES, fill, 0)

    row = lax.broadcasted_iota(jnp.int32, (BF16_ROWS, tq), 0)
    q0 = (row * 0 + qi * tq).astype(F32)
    slope_t = jnp.concatenate([slope] * (tq // LANES), axis=1)
    ext = _select_rows(row, (1.0, 1.0, -slope_t * q0)).astype(BF16)
    qt = qt_ref[...]
    first = lax.broadcasted_iota(jnp.int32, qt.shape, 0) < dh
    zero = jnp.zeros_like(qt)
    for qa_ref, comp in ((qa0_ref, jnp.where(first, qt, zero)), (qa1_ref, jnp.where(first, zero, qt))):
        qa_ref[:HEAD_W, :] = comp
        qa_ref[HEAD_W:HEAD_W + BF16_ROWS, :] = ext
        qa_ref[HEAD_W + BF16_ROWS:, :] = jnp.zeros((AUG - HEAD_W - BF16_ROWS, tq), BF16)
    _init_stats((m0_ref, l0_ref, a0_ref), (m1_ref, l1_ref, a1_ref))

    def load(kb):
        ks = pl.multiple_of(kb * tk, tk)
        return kaug_ref[pl.ds(ks, tk), :], vt_ref[kb]

    def make(blk, c, mask_off):
        return [_Chain(blk[0], blk[1], qa0_ref, (m0_ref, l0_ref, a0_ref), c, LOG2E, mask_off),
                _Chain(blk[0], blk[1], qa1_ref, (m1_ref, l1_ref, a1_ref), c, LOG2E, mask_off)]

    _causal_sweep(qi, tq, tk, load, make)

    ot = a0_ref[...] * (1.0 / l0_ref[...]) - lam * (a1_ref[...] * (1.0 / l1_ref[...]))
    ms = jnp.mean(ot * ot, axis=0, keepdims=True)
    o = (ot * lax.rsqrt(ms + EPS)).T
    o = o * og_ref[...] * out_scale
    o_ref[...] = (o * jax.nn.silu(z_ref[...])).astype(BF16)


def _diff_attention(qt, k, vt, head_params, out_gain, z, *, tq, tk, out_scale):
    batch, width, seq = qt.shape
    heads = width // HEAD_W
    stat = lambda: pltpu.VMEM((1, tq), F32)
    accum = lambda: pltpu.VMEM((HEAD_W, tq), F32)
    return pl.pallas_call(
        functools.partial(_diff_kernel, tq=tq, tk=tk, out_scale=out_scale),
        grid=(batch, heads, seq // tq),
        in_specs=[
            pl.BlockSpec((None, HEAD_W, tq), lambda b, h, q: (b, h, q)),
            pl.BlockSpec((None, seq, HEAD_W), lambda b, h, q: (b, 0, h)),
            pl.BlockSpec((None, seq // tk, HEAD_W, tk), lambda b, h, q: (b, 0, h, 0)),
            pl.BlockSpec((None, 8, LANES), lambda b, h, q: (h, 0, 0)),
            pl.BlockSpec((1, HEAD_W), lambda b, h, q: (0, 0)),
            pl.BlockSpec((None, tq, HEAD_W), lambda b, h, q: (b, q, h)),
        ],
        out_specs=pl.BlockSpec((None, tq, HEAD_W), lambda b, h, q: (b, q, h)),
        out_shape=jax.ShapeDtypeStruct((batch, seq, width), BF16),
        scratch_shapes=[pltpu.VMEM((seq, AUG), BF16),
                        pltpu.VMEM((AUG, tq), BF16), pltpu.VMEM((AUG, tq), BF16),
                        stat(), stat(), accum(), stat(), stat(), accum()],
        compiler_params=_params(("arbitrary", "arbitrary", "arbitrary")),
        name="diff_attention",
    )(qt, k, vt, head_params, out_gain, z)


def _cumsum_kernel(lf_ref, o_ref, carry_ref, *, rows):
    @pl.when(pl.program_id(1) == 0)
    def _():
        carry_ref[...] = jnp.zeros_like(carry_ref)

    x = lf_ref[...]
    r = lax.broadcasted_iota(jnp.int32, (rows, rows), 0)
    c = lax.broadcasted_iota(jnp.int32, (rows, rows), 1)
    tri = (c <= r).astype(BF16)
    hi = x.astype(BF16)
    r1 = x - hi.astype(F32)
    mid = r1.astype(BF16)
    lo = (r1 - mid.astype(F32)).astype(BF16)
    cum = (jnp.dot(tri, hi, preferred_element_type=F32)
           + jnp.dot(tri, mid, preferred_element_type=F32)
           + jnp.dot(tri, lo, preferred_element_type=F32)) + carry_ref[...]
    carry_ref[...] = cum[rows - 1:rows, :]
    o_ref[...] = cum.T


def _cumsum_t(logf, *, rows):
    batch, seq, _ = logf.shape
    return pl.pallas_call(
        functools.partial(_cumsum_kernel, rows=rows),
        grid=(batch, seq // rows),
        in_specs=[pl.BlockSpec((None, rows, LANES), lambda b, i: (b, i, 0))],
        out_specs=pl.BlockSpec((None, LANES, rows), lambda b, i: (b, 0, i)),
        out_shape=jax.ShapeDtypeStruct((batch, LANES, seq), F32),
        scratch_shapes=[pltpu.VMEM((1, LANES), F32)],
        compiler_params=_params(("arbitrary", "arbitrary")),
        name="logf_cumsum",
    )(logf)


def _lane_col(v):
    return jnp.broadcast_to(v.astype(F32)[:, None], (v.shape[0], LANES))


def _alibi_slopes(n):
    return jnp.asarray([2.0 ** (-8.0 * (h + 1) / n) for h in range(n)], F32)


def _even_layer(x2d, layer, batch, seq, tiles, e_norm, e_w_in, e_w_out, lam_re, lam_im, log_dt,
                b_re, b_im, c_re, c_im, d, w_glu, b_glu, q_norm, k_norm, lq1, lk1, lq2, lk2,
                out_norm):
    tm, tq, tk, chunk = tiles
    s5w = S5_GROUPS * S5_GROUP
    lambda_init = 0.8 - 0.6 * math.exp(-0.3 * layer)
    qgain = _lane_col(jnp.tile(q_norm.astype(F32) * (DA_HEAD_DIM ** -0.5), DA_HEADS))
    kgain = _lane_col(jnp.tile(k_norm, DA_HEADS))
    layout = (("f32", 0, 1), ("f32", 1, 1), ("qT", 2, 1), ("k", 3, 1), ("vT", 4, 1), ("f32", 5, 1))
    u, zs, qt, k, vt, zd = _project(
        x2d, e_norm, e_w_in.astype(BF16), qgain, kgain, batch=batch, seq=seq, tm=tm,
        layout=layout, head_dim=DA_HEAD_DIM)

    bmat, cmat, a_re, a_im = _s5_prepare(lam_re, lam_im, log_dt, b_re, b_im, c_re, c_im)
    y_s5 = _s5_mixer(u.reshape(batch, seq, s5w), zs.reshape(batch, seq, s5w), bmat, cmat,
                     a_re, a_im, d, w_glu.astype(BF16), b_glu, chunk=chunk)

    lam = (jnp.exp(jnp.sum(lq1.astype(F32) * lk1.astype(F32)))
           - jnp.exp(jnp.sum(lq2.astype(F32) * lk2.astype(F32))) + lambda_init)
    head_params = jnp.zeros((DA_HEADS, 8, LANES), F32)
    head_params = head_params.at[:, 0, :].set(_alibi_slopes(DA_HEADS)[:, None])
    head_params = head_params.at[:, 1, :].set(lam)
    y_da = _diff_attention(qt, k.reshape(batch, seq, -1), vt, head_params,
                           out_norm.astype(F32).reshape(1, HEAD_W),
                           zd.reshape(batch, seq, -1), tq=tq, tk=tk,
                           out_scale=1.0 - lambda_init)

    w_out = e_w_out.astype(BF16)
    return _out_project([y_s5.reshape(batch * seq, s5w), y_da.reshape(batch * seq, -1)],
                        [w_out[:s5w], w_out[s5w:]], x2d, tm=tm)


def _odd_layer(x2d, batch, seq, tiles, o_norm, o_w_in, o_b_f, o_w_out, q_norm, k_norm):
    tm, tq, tk, _ = tiles
    width = FOX_HEADS * HEAD_W
    nsec = width // SEC
    qgain = _lane_col(jnp.tile(q_norm, SEC // HEAD_W))
    kgain = _lane_col(jnp.tile(k_norm, SEC // HEAD_W))
    w = o_w_in.astype(BF16)
    wf = jnp.zeros((w.shape[0], LANES), BF16).at[:, :FOX_HEADS].set(w[:, 4 * width:])
    bf = jnp.zeros((1, LANES), F32).at[0, :FOX_HEADS].set(o_b_f.astype(F32))
    layout = (("qT", 0, nsec), ("k", nsec, nsec), ("vT", 2 * nsec, nsec), ("f32", 3 * nsec, nsec))
    qt, k, vt, z, logf = _project(
        x2d, o_norm, w[:, :4 * width], qgain, kgain, batch=batch, seq=seq, tm=tm,
        layout=layout, head_dim=HEAD_W, wf=wf, bf=bf)
    ct = _cumsum_t(logf.reshape(batch, seq, LANES), rows=tm)
    c2 = ct[:, :FOX_HEADS].reshape(batch, FOX_HEADS, seq // LANES, LANES)
    y = _fox_attention(qt, k.reshape(batch, seq, width), vt, c2, z.reshape(batch, seq, width),
                       tq=tq, tk=tk)
    return _out_project([y.reshape(batch * seq, width)], [o_w_out.astype(BF16)], x2d, tm=tm)


def _tiles(seq):
    tm = min(512, seq)
    return tm, min(1024, seq), tm, min(128, seq)


def kernel(x, e_norm, e_w_in, e_w_out, s5_lambda_re, s5_lambda_im, s5_log_dt, s5_b_re, s5_b_im, s5_c_re, s5_c_im, s5_d, s5_w_glu, s5_b_glu, da_q_norm, da_k_norm, da_lambda_q1, da_lambda_k1, da_lambda_q2, da_lambda_k2, da_out_norm, o_norm, o_w_in, o_b_f, o_w_out, fox_q_norm, fox_k_norm):
    batch, seq, d_model = x.shape
    tiles = _tiles(seq)
    depth = e_norm.shape[0] + o_norm.shape[0]
    x2d = x.reshape(batch * seq, d_model)
    for layer in range(depth):
        j = layer // 2
        if layer % 2 == 0:
            x2d = _even_layer(
                x2d, layer, batch, seq, tiles, e_norm[j], e_w_in[j], e_w_out[j],
                s5_lambda_re[j], s5_lambda_im[j], s5_log_dt[j], s5_b_re[j], s5_b_im[j],
                s5_c_re[j], s5_c_im[j], s5_d[j], s5_w_glu[j], s5_b_glu[j],
                da_q_norm[j], da_k_norm[j], da_lambda_q1[j], da_lambda_k1[j],
                da_lambda_q2[j], da_lambda_k2[j], da_out_norm[j])
        else:
            x2d = _odd_layer(x2d, batch, seq, tiles, o_norm[j], o_w_in[j], o_b_f[j], o_w_out[j],
                             fox_q_norm[j], fox_k_norm[j])
    return x2d.reshape(batch, seq, d_model)
```

```python
import functools
import math

import jax
import jax.numpy as jnp
from jax import lax
from jax.experimental import pallas as pl
from jax.experimental.pallas import tpu as pltpu

F32 = jnp.float32
BF16 = jnp.bfloat16
EPS = 1e-6
NEG = -1e30
LOG2E = math.log2(math.e)

LANES = 128
BF16_ROWS = 16
AUG = 256
CHUNK = 256
SCORES_AHEAD = 3
SEC = 1024
VMEM_LIMIT = 52 * 1024 * 1024

S5_GROUPS = 64
S5_GROUP = 16
S5_STATE = 64
S5_SLICES = 8
DA_HEADS = 8
DA_HEAD_DIM = 64
FOX_HEADS = 16
HEAD_W = 128
ACC_ROWS = HEAD_W + BF16_ROWS


def _params(sem):
    return pltpu.CompilerParams(dimension_semantics=sem, vmem_limit_bytes=VMEM_LIMIT)


def _head_norm_t(acc_t, gain_ref, head_dim):
    tm = acc_t.shape[1]
    r = acc_t.reshape(SEC // head_dim, head_dim, tm)
    ms = jnp.mean(r * r, axis=1, keepdims=True)
    n = (r * lax.rsqrt(ms + EPS)).reshape(SEC, tm)
    g = gain_ref[...]
    return jnp.concatenate(
        [n[:, c * LANES:(c + 1) * LANES] * g for c in range(tm // LANES)], axis=1)


def _proj_kernel(*refs, layout, head_dim, has_logf):
    it = iter(refs)
    x_ref, g_ref, w_ref = next(it), next(it), next(it)
    if has_logf:
        wf_ref, bf_ref = next(it), next(it)
    qg_ref, kg_ref = next(it), next(it)
    out_refs = [next(it) for _ in layout]
    if has_logf:
        logf_ref = next(it)
    hs_ref = next(it)

    j = pl.program_id(1)

    @pl.when(j == 0)
    def _():
        x = x_ref[...]
        ms = jnp.mean(x * x, axis=-1, keepdims=True)
        hs_ref[...] = (x * lax.rsqrt(ms + EPS) * g_ref[...]).astype(BF16)
        if has_logf:
            f = jnp.dot(hs_ref[...], wf_ref[...], preferred_element_type=F32) + bf_ref[...]
            logf_ref[...] = jax.nn.log_sigmoid(f)

    acc = jnp.dot(hs_ref[...], w_ref[...], preferred_element_type=F32)

    for (kind, j0, cnt), o_ref in zip(layout, out_refs):
        @pl.when((j >= j0) & (j < j0 + cnt))
        def _(kind=kind, o_ref=o_ref):
            if kind == "f32":
                o_ref[...] = acc
            elif kind == "vT":
                o_ref[...] = acc.T.astype(BF16)
            elif kind == "qT":
                o_ref[...] = _head_norm_t(acc.T, qg_ref, head_dim).astype(BF16)
            else:
                o_ref[...] = _head_norm_t(acc.T, kg_ref, head_dim).T.astype(BF16)


def _project(x2d, norm_g, w, qgain, kgain, *, batch, seq, tm, layout, head_dim, wf=None, bf=None):
    n_rows, d = x2d.shape
    nsec = w.shape[1] // SEC
    nt = seq // tm
    has_logf = wf is not None

    def sec(j, j0, cnt):
        return jnp.clip(j - j0, 0, cnt - 1)

    in_specs = [
        pl.BlockSpec((tm, d), lambda i, j: (i, 0)),
        pl.BlockSpec((1, d), lambda i, j: (0, 0)),
        pl.BlockSpec((d, SEC), lambda i, j: (0, j)),
    ]
    args = [x2d, norm_g.reshape(1, d), w]
    if has_logf:
        in_specs += [pl.BlockSpec((d, LANES), lambda i, j: (0, 0)),
                     pl.BlockSpec((1, LANES), lambda i, j: (0, 0))]
        args += [wf, bf]
    in_specs += [pl.BlockSpec((SEC, LANES), lambda i, j: (0, 0)),
                 pl.BlockSpec((SEC, LANES), lambda i, j: (0, 0))]
    args += [qgain, kgain]

    out_shapes, out_specs = [], []
    for kind, j0, cnt in layout:
        if kind == "f32" or kind == "k":
            dt = F32 if kind == "f32" else BF16
            out_shapes.append(jax.ShapeDtypeStruct((n_rows, cnt * SEC), dt))
            out_specs.append(pl.BlockSpec(
                (tm, SEC), lambda i, j, j0=j0, cnt=cnt: (i, sec(j, j0, cnt))))
        elif kind == "qT":
            out_shapes.append(jax.ShapeDtypeStruct((batch, cnt * SEC, seq), BF16))
            out_specs.append(pl.BlockSpec(
                (None, SEC, tm), lambda i, j, j0=j0, cnt=cnt: (i // nt, sec(j, j0, cnt), i % nt)))
        else:
            out_shapes.append(jax.ShapeDtypeStruct((batch, nt, cnt * SEC, tm), BF16))
            out_specs.append(pl.BlockSpec(
                (None, None, SEC, tm),
                lambda i, j, j0=j0, cnt=cnt: (i // nt, i % nt, sec(j, j0, cnt), 0)))
    if has_logf:
        out_shapes.append(jax.ShapeDtypeStruct((n_rows, LANES), F32))
        out_specs.append(pl.BlockSpec((tm, LANES), lambda i, j: (i, 0)))

    return pl.pallas_call(
        functools.partial(_proj_kernel, layout=layout, head_dim=head_dim, has_logf=has_logf),
        grid=(n_rows // tm, nsec),
        in_specs=in_specs,
        out_specs=out_specs,
        out_shape=out_shapes,
        scratch_shapes=[pltpu.VMEM((tm, d), BF16)],
        compiler_params=_params(("arbitrary", "arbitrary")),
        name="rms_in_proj",
    )(*args)


def _out_kernel(*refs, n_in):
    ys, ws = refs[:n_in], refs[n_in:2 * n_in]
    x_ref, o_ref = refs[2 * n_in], refs[2 * n_in + 1]
    acc = x_ref[...]
    for y_ref, w_ref in zip(ys, ws):
        acc = acc + jnp.dot(y_ref[...], w_ref[...], preferred_element_type=F32)
    o_ref[...] = acc


def _out_project(ys, ws, x2d, *, tm):
    n_rows, d = x2d.shape
    n_in = len(ys)
    in_specs = [pl.BlockSpec((tm, y.shape[1]), lambda i: (i, 0)) for y in ys]
    in_specs += [pl.BlockSpec(w.shape, lambda i: (0, 0)) for w in ws]
    in_specs += [pl.BlockSpec((tm, d), lambda i: (i, 0))]
    return pl.pallas_call(
        functools.partial(_out_kernel, n_in=n_in),
        grid=(n_rows // tm,),
        in_specs=in_specs,
        out_specs=pl.BlockSpec((tm, d), lambda i: (i, 0)),
        out_shape=jax.ShapeDtypeStruct((n_rows, d), F32),
        compiler_params=_params(("arbitrary",)),
        name="out_proj_residual",
    )(*ys, *ws, x2d)


def _s5_prepare(lam_re, lam_im, log_dt, b_re, b_im, c_re, c_im):
    dt = jnp.exp(log_dt.astype(F32))[:, None]
    lr, li = lam_re.astype(F32), lam_im.astype(F32)
    mag = jnp.exp(lr * dt)
    ar, ai = mag * jnp.cos(li * dt), mag * jnp.sin(li * dt)
    den = lr * lr + li * li
    nr, ni = ar - 1.0, ai
    kr = (nr * lr + ni * li) / den
    ki = (ni * lr - nr * li) / den
    br, bi = b_re.astype(F32), b_im.astype(F32)
    bbr = kr[..., None] * br - ki[..., None] * bi
    bbi = kr[..., None] * bi + ki[..., None] * br
    nblk = S5_GROUPS // 16
    eye = jnp.eye(16, dtype=F32)

    def b_tiles(bb):
        t = bb.reshape(nblk, 16, S5_SLICES, 8, S5_GROUP)
        return jnp.einsum("ngkph,fg->knfhgp", t, eye).reshape(S5_SLICES, nblk, 256, LANES)

    def c_tiles(cc):
        t = cc.reshape(nblk, 16, S5_GROUP, S5_SLICES, 8)
        return jnp.einsum("nghkp,fg->nkgpfh", t, eye).reshape(nblk, S5_SLICES, LANES, 256)

    bmat = jnp.concatenate([b_tiles(bbr), b_tiles(bbi)], axis=-1).astype(BF16)
    cmat = jnp.concatenate([c_tiles(c_re.astype(F32)), c_tiles(-c_im.astype(F32))], axis=2)
    cmat = cmat.reshape(nblk, S5_SLICES * 256, 256).astype(BF16)

    def a_tiles(a):
        t = a.reshape(nblk, 16, S5_SLICES, 8).transpose(2, 0, 1, 3)
        return t.reshape(S5_SLICES, nblk * LANES)

    return bmat, cmat, a_tiles(ar), a_tiles(ai)


def _s5_kernel(u_ref, z_ref, bm_ref, cm_ref, ar_ref, ai_ref, d_ref, wg_ref, bg_ref,
               o_ref, s_ref, h_ref, *, chunk, pitch):
    nb = u_ref.shape[0]
    nblk = ar_ref.shape[1] // LANES

    @pl.when(pl.program_id(0) == 0)
    def _():
        h_ref[...] = jnp.zeros_like(h_ref)

    for b in range(nb):
        ub = u_ref[b].astype(BF16)
        for k in range(S5_SLICES):
            for n in range(nblk):
                x = jnp.dot(ub[:, n * 256:(n + 1) * 256], bm_ref[k, n],
                            preferred_element_type=F32)
                s_ref[b, 2 * n, k * pitch:k * pitch + chunk, :] = x[:, :LANES]
                s_ref[b, 2 * n + 1, k * pitch:k * pitch + chunk, :] = x[:, LANES:]

    ar = [ar_ref[:, n * LANES:(n + 1) * LANES] for n in range(nblk)]
    ai = [ai_ref[:, n * LANES:(n + 1) * LANES] for n in range(nblk)]

    def step(t, hs):
        out = []
        for b in range(nb):
            for n in range(nblk):
                hr, hi = hs[2 * (b * nblk + n)], hs[2 * (b * nblk + n) + 1]
                rows = pl.ds(t, S5_SLICES, stride=pitch)
                xr = s_ref[b, 2 * n, rows, :]
                xi = s_ref[b, 2 * n + 1, rows, :]
                nr = ar[n] * hr - ai[n] * hi + xr
                ni = ar[n] * hi + ai[n] * hr + xi
                s_ref[b, 2 * n, rows, :] = nr
                s_ref[b, 2 * n + 1, rows, :] = ni
                out += [nr, ni]
        return tuple(out)

    h0 = tuple(h_ref[b, s] for b in range(nb) for s in range(2 * nblk))
    hs = lax.fori_loop(0, chunk, step, h0, unroll=4)
    for b in range(nb):
        for s in range(2 * nblk):
            h_ref[b, s] = hs[b * 2 * nblk + s]

    for b in range(nb):
        ys = []
        for n in range(nblk):
            acc = None
            for k in range(S5_SLICES):
                rows = slice(k * pitch, k * pitch + chunk)
                hk = jnp.concatenate([s_ref[b, 2 * n, rows, :], s_ref[b, 2 * n + 1, rows, :]],
                                     axis=1).astype(BF16)
                part = jnp.dot(hk, cm_ref[n, k * 256:(k + 1) * 256, :],
                               preferred_element_type=F32)
                acc = part if acc is None else acc + part
            ys.append(acc)
        y = jnp.concatenate(ys, axis=1) + d_ref[...] * u_ref[b]
        y = jax.nn.gelu(y)
        gate = jax.nn.sigmoid(
            jnp.dot(y.astype(BF16), wg_ref[...], preferred_element_type=F32) + bg_ref[...])
        o_ref[b] = (y * gate * jax.nn.silu(z_ref[b])).astype(BF16)


def _s5_mixer(u, z, bmat, cmat, a_re, a_im, d, w_glu, b_glu, *, chunk):
    batch, seq, width = u.shape
    pitch = chunk + 8
    nblk = width // 256
    const = lambda shape: pl.BlockSpec(shape, lambda c: (0,) * len(shape))
    return pl.pallas_call(
        functools.partial(_s5_kernel, chunk=chunk, pitch=pitch),
        grid=(seq // chunk,),
        in_specs=[
            pl.BlockSpec((batch, chunk, width), lambda c: (0, c, 0)),
            pl.BlockSpec((batch, chunk, width), lambda c: (0, c, 0)),
            const(bmat.shape), const(cmat.shape), const(a_re.shape), const(a_im.shape),
            const((1, width)), const(w_glu.shape), const((1, width)),
        ],
        out_specs=pl.BlockSpec((batch, chunk, width), lambda c: (0, c, 0)),
        out_shape=jax.ShapeDtypeStruct((batch, seq, width), BF16),
        scratch_shapes=[
            pltpu.VMEM((batch, 2 * nblk, S5_SLICES * pitch, LANES), F32),
            pltpu.VMEM((batch, 2 * nblk, S5_SLICES, LANES), F32),
        ],
        compiler_params=_params(("arbitrary",)),
        name="s5_mixer",
    )(u, z, bmat, cmat, a_re, a_im, d.reshape(1, width), w_glu, b_glu.reshape(1, width))


def _split3(x):
    hi = x.astype(BF16).astype(F32)
    r = x - hi
    mid = r.astype(BF16).astype(F32)
    return hi, mid, r - mid


def _select_rows(index, values):
    out = jnp.zeros(index.shape, F32)
    for i, v in enumerate(values):
        out = jnp.where(index == i, v, out)
    return out


class _Chain:
    def __init__(self, kaug_blk, vt_blk, qaug_ref, stats, c, mult, mask_off):
        self.kaug_blk, self.vt_blk, self.qaug_ref = kaug_blk, vt_blk, qaug_ref
        self.m_ref, self.acc_ref = stats
        self.lanes = slice(c * CHUNK, (c + 1) * CHUNK)
        self.mult, self.mask_off = mult, mask_off

    def scores(self):
        s = jnp.dot(self.kaug_blk, self.qaug_ref[:, self.lanes],
                    preferred_element_type=F32)
        if self.mask_off is not None:
            key = lax.broadcasted_iota(jnp.int32, s.shape, 0)
            qry = lax.broadcasted_iota(jnp.int32, s.shape, 1)
            s = jnp.where(key - qry <= self.mask_off, s, NEG)
        return s * self.mult

    def softmax(self, t):
        m_prev = self.m_ref[:, self.lanes]
        m_new = jnp.maximum(m_prev, jnp.max(t, axis=0, keepdims=True))
        alpha = jnp.exp2(m_prev - m_new)
        p = jnp.exp2(t - m_new)
        self.m_ref[:, self.lanes] = m_new
        return p.astype(BF16), alpha

    def values(self, p, alpha):
        pv = jnp.dot(self.vt_blk, p, preferred_element_type=F32)
        self.acc_ref[:, self.lanes] = alpha * self.acc_ref[:, self.lanes] + pv


def _run_chains(chains):
    ready = [ch.scores() for ch in chains[:SCORES_AHEAD]]
    for i, ch in enumerate(chains):
        if i + SCORES_AHEAD < len(chains):
            ready.append(chains[i + SCORES_AHEAD].scores())
        p, alpha = ch.softmax(ready.pop(0))
        ch.values(p, alpha)


def _causal_sweep(qi, tq, tk, load, make):
    per, nch = tq // tk, tq // CHUNK

    def body(j, carry):
        chains = []
        for d in range(per):
            blk = load(j * per + d)
            for c in range(nch):
                chains += make(blk, c, None)
        _run_chains(chains)
        return carry

    lax.fori_loop(0, qi, body, 0)
    chains = []
    for d in range(per):
        blk = load(qi * per + d)
        for c in range(nch):
            off = c * CHUNK - d * tk
            if off < -(CHUNK - 1):
                continue
            chains += make(blk, c, None if off >= tk - 1 else off)
    _run_chains(chains)


def _init_stats(*pairs):
    for m_ref, a_ref in pairs:
        m_ref[...] = jnp.full_like(m_ref, NEG)
        a_ref[...] = jnp.zeros_like(a_ref)


def _values_with_ones(vt_blk):
    row = lax.broadcasted_iota(jnp.int32, (BF16_ROWS, vt_blk.shape[1]), 0)
    ones = jnp.where(row == 0, 1.0, 0.0).astype(BF16)
    return jnp.concatenate([vt_blk, ones], axis=0)


def _normalised(acc_ref):
    return acc_ref[:HEAD_W, :] * (1.0 / acc_ref[HEAD_W:HEAD_W + 1, :])


def _fox_kernel(qt_ref, k_ref, vt_ref, c_ref, z_ref, o_ref,
                kaug_ref, qaug_ref, m_ref, acc_ref, *, tq, tk, scale):
    qi = pl.program_id(2)
    inv = 1.0 / scale

    @pl.when(qi == 0)
    def _():
        lane = lax.broadcasted_iota(jnp.int32, (LANES, LANES), 1)

        def fill(r, carry):
            rows = pl.ds(pl.multiple_of(r * LANES, LANES), LANES)
            col = jnp.broadcast_to(c_ref[pl.ds(r, 1), :], (LANES, LANES)).T
            hi, mid, lo = _split3(col * (-inv))
            ext = _select_rows(lane, (hi, mid, lo, 1.0, 1.0, 1.0))
            kaug_ref[rows, :LANES] = k_ref[rows, :]
            kaug_ref[rows, LANES:] = ext.astype(BF16)
            return carry

        lax.fori_loop(0, c_ref.shape[0], fill, 0)

    crow = c_ref[pl.ds(qi * (tq // LANES), 1), :]
    c0 = jnp.broadcast_to(crow[:, 0:1], (BF16_ROWS, tq)) * inv
    hi, mid, lo = _split3(c0)
    row = lax.broadcasted_iota(jnp.int32, (BF16_ROWS, tq), 0)
    qaug_ref[:HEAD_W, :] = qt_ref[...]
    qaug_ref[HEAD_W:HEAD_W + BF16_ROWS, :] = _select_rows(row, (1.0, 1.0, 1.0, hi, mid, lo)).astype(BF16)
    qaug_ref[HEAD_W + BF16_ROWS:, :] = jnp.zeros((AUG - HEAD_W - BF16_ROWS, tq), BF16)
    _init_stats((m_ref, acc_ref))

    def load(kb):
        ks = pl.multiple_of(kb * tk, tk)
        return kaug_ref[pl.ds(ks, tk), :], _values_with_ones(vt_ref[kb])

    def make(blk, c, mask_off):
        return [_Chain(blk[0], blk[1], qaug_ref, (m_ref, acc_ref), c, scale * LOG2E, mask_off)]

    _causal_sweep(qi, tq, tk, load, make)

    o = _normalised(acc_ref).T
    o_ref[...] = (o * jax.nn.silu(z_ref[...])).astype(BF16)


def _fox_attention(qt, k, vt, c2, z, *, tq, tk):
    batch, width, seq = qt.shape
    heads = width // HEAD_W
    scale = HEAD_W ** -0.5
    return pl.pallas_call(
        functools.partial(_fox_kernel, tq=tq, tk=tk, scale=scale),
        grid=(batch, heads, seq // tq),
        in_specs=[
            pl.BlockSpec((None, HEAD_W, tq), lambda b, h, q: (b, h, q)),
            pl.BlockSpec((None, seq, HEAD_W), lambda b, h, q: (b, 0, h)),
            pl.BlockSpec((None, seq // tk, HEAD_W, tk), lambda b, h, q: (b, 0, h, 0)),
            pl.BlockSpec((None, None, seq // LANES, LANES), lambda b, h, q: (b, h, 0, 0)),
            pl.BlockSpec((None, tq, HEAD_W), lambda b, h, q: (b, q, h)),
        ],
        out_specs=pl.BlockSpec((None, tq, HEAD_W), lambda b, h, q: (b, q, h)),
        out_shape=jax.ShapeDtypeStruct((batch, seq, width), BF16),
        scratch_shapes=[
            pltpu.VMEM((seq, AUG), BF16), pltpu.VMEM((AUG, tq), BF16),
            pltpu.VMEM((1, tq), F32), pltpu.VMEM((ACC_ROWS, tq), F32),
        ],
        compiler_params=_params(("arbitrary", "arbitrary", "arbitrary")),
        name="fox_attention",
    )(qt, k, vt, c2, z)


def _diff_kernel(qt_ref, k_ref, vt_ref, hp_ref, og_ref, z_ref, o_ref,
                 kaug_ref, qa0_ref, qa1_ref, m0_ref, a0_ref, m1_ref, a1_ref,
                 *, tq, tk, out_scale):
    qi = pl.program_id(2)
    dh = qt_ref.shape[0] // 2
    slope = hp_ref[0:1, :]
    lam = hp_ref[1:2, 0:1]

    @pl.when(qi == 0)
    def _():
        lane = lax.broadcasted_iota(jnp.int32, (LANES, LANES), 1)
        key = lax.broadcasted_iota(jnp.int32, (LANES, LANES), 0)

        def fill(r, carry):
            rows = pl.ds(pl.multiple_of(r * LANES, LANES), LANES)
            base = (key * 0 + r * LANES).astype(F32)
            ext = _select_rows(lane, (slope * base, slope * key.astype(F32), 1.0))
            kaug_ref[rows, :LANES] = k_ref[rows, :]
            kaug_ref[rows, LANES:] = ext.astype(BF16)
            return carry

        lax.fori_loop(0, k_ref.shape[0] // LANES, fill, 0)

    row = lax.broadcasted_iota(jnp.int32, (BF16_ROWS, tq), 0)
    q0 = (row * 0 + qi * tq).astype(F32)
    slope_t = jnp.concatenate([slope] * (tq // LANES), axis=1)
    ext = _select_rows(row, (1.0, 1.0, -slope_t * q0)).astype(BF16)
    qt = qt_ref[...]
    first = lax.broadcasted_iota(jnp.int32, qt.shape, 0) < dh
    zero = jnp.zeros_like(qt)
    for qa_ref, comp in ((qa0_ref, jnp.where(first, qt, zero)), (qa1_ref, jnp.where(first, zero, qt))):
        qa_ref[:HEAD_W, :] = comp
        qa_ref[HEAD_W:HEAD_W + BF16_ROWS, :] = ext
        qa_ref[HEAD_W + BF16_ROWS:, :] = jnp.zeros((AUG - HEAD_W - BF16_ROWS, tq), BF16)
    _init_stats((m0_ref, a0_ref), (m1_ref, a1_ref))

    def load(kb):
        ks = pl.multiple_of(kb * tk, tk)
        return kaug_ref[pl.ds(ks, tk), :], _values_with_ones(vt_ref[kb])

    def make(blk, c, mask_off):
        return [_Chain(blk[0], blk[1], qa0_ref, (m0_ref, a0_ref), c, LOG2E, mask_off),
                _Chain(blk[0], blk[1], qa1_ref, (m1_ref, a1_ref), c, LOG2E, mask_off)]

    _causal_sweep(qi, tq, tk, load, make)

    ot = _normalised(a0_ref) - lam * _normalised(a1_ref)
    ms = jnp.mean(ot * ot, axis=0, keepdims=True)
    o = (ot * lax.rsqrt(ms + EPS)).T
    o = o * og_ref[...] * out_scale
    o_ref[...] = (o * jax.nn.silu(z_ref[...])).astype(BF16)


def _diff_attention(qt, k, vt, head_params, out_gain, z, *, tq, tk, out_scale):
    batch, width, seq = qt.shape
    heads = width // HEAD_W
    stat = lambda: pltpu.VMEM((1, tq), F32)
    accum = lambda: pltpu.VMEM((ACC_ROWS, tq), F32)
    return pl.pallas_call(
        functools.partial(_diff_kernel, tq=tq, tk=tk, out_scale=out_scale),
        grid=(batch, heads, seq // tq),
        in_specs=[
            pl.BlockSpec((None, HEAD_W, tq), lambda b, h, q: (b, h, q)),
            pl.BlockSpec((None, seq, HEAD_W), lambda b, h, q: (b, 0, h)),
            pl.BlockSpec((None, seq // tk, HEAD_W, tk), lambda b, h, q: (b, 0, h, 0)),
            pl.BlockSpec((None, 8, LANES), lambda b, h, q: (h, 0, 0)),
            pl.BlockSpec((1, HEAD_W), lambda b, h, q: (0, 0)),
            pl.BlockSpec((None, tq, HEAD_W), lambda b, h, q: (b, q, h)),
        ],
        out_specs=pl.BlockSpec((None, tq, HEAD_W), lambda b, h, q: (b, q, h)),
        out_shape=jax.ShapeDtypeStruct((batch, seq, width), BF16),
        scratch_shapes=[pltpu.VMEM((seq, AUG), BF16),
                        pltpu.VMEM((AUG, tq), BF16), pltpu.VMEM((AUG, tq), BF16),
                        stat(), accum(), stat(), accum()],
        compiler_params=_params(("arbitrary", "arbitrary", "arbitrary")),
        name="diff_attention",
    )(qt, k, vt, head_params, out_gain, z)


def _cumsum_kernel(lf_ref, o_ref, carry_ref, *, rows):
    @pl.when(pl.program_id(1) == 0)
    def _():
        carry_ref[...] = jnp.zeros_like(carry_ref)

    x = lf_ref[...]
    r = lax.broadcasted_iota(jnp.int32, (rows, rows), 0)
    c = lax.broadcasted_iota(jnp.int32, (rows, rows), 1)
    tri = (c <= r).astype(BF16)
    hi = x.astype(BF16)
    r1 = x - hi.astype(F32)
    mid = r1.astype(BF16)
    lo = (r1 - mid.astype(F32)).astype(BF16)
    cum = (jnp.dot(tri, hi, preferred_element_type=F32)
           + jnp.dot(tri, mid, preferred_element_type=F32)
           + jnp.dot(tri, lo, preferred_element_type=F32)) + carry_ref[...]
    carry_ref[...] = cum[rows - 1:rows, :]
    o_ref[...] = cum.T


def _cumsum_t(logf, *, rows):
    batch, seq, _ = logf.shape
    return pl.pallas_call(
        functools.partial(_cumsum_kernel, rows=rows),
        grid=(batch, seq // rows),
        in_specs=[pl.BlockSpec((None, rows, LANES), lambda b, i: (b, i, 0))],
        out_specs=pl.BlockSpec((None, LANES, rows), lambda b, i: (b, 0, i)),
        out_shape=jax.ShapeDtypeStruct((batch, LANES, seq), F32),
        scratch_shapes=[pltpu.VMEM((1, LANES), F32)],
        compiler_params=_params(("arbitrary", "arbitrary")),
        name="logf_cumsum",
    )(logf)


def _lane_col(v):
    return jnp.broadcast_to(v.astype(F32)[:, None], (v.shape[0], LANES))


def _alibi_slopes(n):
    return jnp.asarray([2.0 ** (-8.0 * (h + 1) / n) for h in range(n)], F32)


def _even_layer(x2d, layer, batch, seq, tiles, e_norm, e_w_in, e_w_out, lam_re, lam_im, log_dt,
                b_re, b_im, c_re, c_im, d, w_glu, b_glu, q_norm, k_norm, lq1, lk1, lq2, lk2,
                out_norm):
    tm, tq, tk, chunk = tiles
    s5w = S5_GROUPS * S5_GROUP
    lambda_init = 0.8 - 0.6 * math.exp(-0.3 * layer)
    qgain = _lane_col(jnp.tile(q_norm.astype(F32) * (DA_HEAD_DIM ** -0.5), DA_HEADS))
    kgain = _lane_col(jnp.tile(k_norm, DA_HEADS))
    layout = (("f32", 0, 1), ("f32", 1, 1), ("qT", 2, 1), ("k", 3, 1), ("vT", 4, 1), ("f32", 5, 1))
    u, zs, qt, k, vt, zd = _project(
        x2d, e_norm, e_w_in.astype(BF16), qgain, kgain, batch=batch, seq=seq, tm=tm,
        layout=layout, head_dim=DA_HEAD_DIM)

    bmat, cmat, a_re, a_im = _s5_prepare(lam_re, lam_im, log_dt, b_re, b_im, c_re, c_im)
    y_s5 = _s5_mixer(u.reshape(batch, seq, s5w), zs.reshape(batch, seq, s5w), bmat, cmat,
                     a_re, a_im, d, w_glu.astype(BF16), b_glu, chunk=chunk)

    lam = (jnp.exp(jnp.sum(lq1.astype(F32) * lk1.astype(F32)))
           - jnp.exp(jnp.sum(lq2.astype(F32) * lk2.astype(F32))) + lambda_init)
    head_params = jnp.zeros((DA_HEADS, 8, LANES), F32)
    head_params = head_params.at[:, 0, :].set(_alibi_slopes(DA_HEADS)[:, None])
    head_params = head_params.at[:, 1, :].set(lam)
    y_da = _diff_attention(qt, k.reshape(batch, seq, -1), vt, head_params,
                           out_norm.astype(F32).reshape(1, HEAD_W),
                           zd.reshape(batch, seq, -1), tq=tq, tk=tk,
                           out_scale=1.0 - lambda_init)

    w_out = e_w_out.astype(BF16)
    return _out_project([y_s5.reshape(batch * seq, s5w), y_da.reshape(batch * seq, -1)],
                        [w_out[:s5w], w_out[s5w:]], x2d, tm=tm)


def _odd_layer(x2d, batch, seq, tiles, o_norm, o_w_in, o_b_f, o_w_out, q_norm, k_norm):
    tm, tq, tk, _ = tiles
    width = FOX_HEADS * HEAD_W
    nsec = width // SEC
    qgain = _lane_col(jnp.tile(q_norm, SEC // HEAD_W))
    kgain = _lane_col(jnp.tile(k_norm, SEC // HEAD_W))
    w = o_w_in.astype(BF16)
    wf = jnp.zeros((w.shape[0], LANES), BF16).at[:, :FOX_HEADS].set(w[:, 4 * width:])
    bf = jnp.zeros((1, LANES), F32).at[0, :FOX_HEADS].set(o_b_f.astype(F32))
    layout = (("qT", 0, nsec), ("k", nsec, nsec), ("vT", 2 * nsec, nsec), ("f32", 3 * nsec, nsec))
    qt, k, vt, z, logf = _project(
        x2d, o_norm, w[:, :4 * width], qgain, kgain, batch=batch, seq=seq, tm=tm,
        layout=layout, head_dim=HEAD_W, wf=wf, bf=bf)
    ct = _cumsum_t(logf.reshape(batch, seq, LANES), rows=tm)
    c2 = ct[:, :FOX_HEADS].reshape(batch, FOX_HEADS, seq // LANES, LANES)
    y = _fox_attention(qt, k.reshape(batch, seq, width), vt, c2, z.reshape(batch, seq, width),
                       tq=tq, tk=tk)
    return _out_project([y.reshape(batch * seq, width)], [o_w_out.astype(BF16)], x2d, tm=tm)


def _tiles(seq):
    tm = min(512, seq)
    return tm, min(1024, seq), tm, min(128, seq)


def kernel(x, e_norm, e_w_in, e_w_out, s5_lambda_re, s5_lambda_im, s5_log_dt, s5_b_re, s5_b_im, s5_c_re, s5_c_im, s5_d, s5_w_glu, s5_b_glu, da_q_norm, da_k_norm, da_lambda_q1, da_lambda_k1, da_lambda_q2, da_lambda_k2, da_out_norm, o_norm, o_w_in, o_b_f, o_w_out, fox_q_norm, fox_k_norm):
    batch, seq, d_model = x.shape
    tiles = _tiles(seq)
    depth = e_norm.shape[0] + o_norm.shape[0]
    x2d = x.reshape(batch * seq, d_model)
    for layer in range(depth):
        j = layer // 2
        if layer % 2 == 0:
            x2d = _even_layer(
                x2d, layer, batch, seq, tiles, e_norm[j], e_w_in[j], e_w_out[j],
                s5_lambda_re[j], s5_lambda_im[j], s5_log_dt[j], s5_b_re[j], s5_b_im[j],
                s5_c_re[j], s5_c_im[j], s5_d[j], s5_w_glu[j], s5_b_glu[j],
                da_q_norm[j], da_k_norm[j], da_lambda_q1[j], da_lambda_k1[j],
                da_lambda_q2[j], da_lambda_k2[j], da_out_norm[j])
        else:
            x2d = _odd_layer(x2d, batch, seq, tiles, o_norm[j], o_w_in[j], o_b_f[j], o_w_out[j],
                             fox_q_norm[j], fox_k_norm[j])
    return x2d.reshape(batch, seq, d_model)
```

```python
import functools
import math

import jax
import jax.numpy as jnp
from jax import lax
from jax.experimental import pallas as pl
from jax.experimental.pallas import tpu as pltpu

F32 = jnp.float32
BF16 = jnp.bfloat16
EPS = 1e-6
NEG = -1e30
LOG2E = math.log2(math.e)

LANES = 128
BF16_ROWS = 16
AUG = 256
CHUNK = 256
SCORES_AHEAD = 3
SEC = 1024
VMEM_LIMIT = 52 * 1024 * 1024

S5_GROUPS = 64
S5_GROUP = 16
S5_STATE = 64
S5_SLICES = 8
DA_HEADS = 8
DA_HEAD_DIM = 64
FOX_HEADS = 16
HEAD_W = 128
ACC_ROWS = HEAD_W + BF16_ROWS


def _params(sem):
    return pltpu.CompilerParams(dimension_semantics=sem, vmem_limit_bytes=VMEM_LIMIT)


def _head_norm_t(acc_t, gain_ref, head_dim):
    tm = acc_t.shape[1]
    r = acc_t.reshape(SEC // head_dim, head_dim, tm)
    ms = jnp.mean(r * r, axis=1, keepdims=True)
    n = (r * lax.rsqrt(ms + EPS)).reshape(SEC, tm)
    g = gain_ref[...]
    return jnp.concatenate(
        [n[:, c * LANES:(c + 1) * LANES] * g for c in range(tm // LANES)], axis=1)


def _proj_kernel(*refs, layout, head_dim, has_logf):
    it = iter(refs)
    x_ref, g_ref, w_ref = next(it), next(it), next(it)
    if has_logf:
        wf_ref, bf_ref = next(it), next(it)
    qg_ref, kg_ref = next(it), next(it)
    out_refs = [next(it) for _ in layout]
    if has_logf:
        logf_ref = next(it)
    hs_ref = next(it)

    j = pl.program_id(1)

    @pl.when(j == 0)
    def _():
        x = x_ref[...]
        ms = jnp.mean(x * x, axis=-1, keepdims=True)
        hs_ref[...] = (x * lax.rsqrt(ms + EPS) * g_ref[...]).astype(BF16)
        if has_logf:
            f = jnp.dot(hs_ref[...], wf_ref[...], preferred_element_type=F32) + bf_ref[...]
            logf_ref[...] = jax.nn.log_sigmoid(f)

    acc = jnp.dot(hs_ref[...], w_ref[...], preferred_element_type=F32)

    for (kind, j0, cnt), o_ref in zip(layout, out_refs):
        @pl.when((j >= j0) & (j < j0 + cnt))
        def _(kind=kind, o_ref=o_ref):
            if kind == "f32":
                o_ref[...] = acc
            elif kind == "vT":
                o_ref[...] = acc.T.astype(BF16)
            elif kind == "qT":
                o_ref[...] = _head_norm_t(acc.T, qg_ref, head_dim).astype(BF16)
            else:
                o_ref[...] = _head_norm_t(acc.T, kg_ref, head_dim).T.astype(BF16)


def _project(x2d, norm_g, w, qgain, kgain, *, batch, seq, tm, layout, head_dim, wf=None, bf=None):
    n_rows, d = x2d.shape
    nsec = w.shape[1] // SEC
    nt = seq // tm
    has_logf = wf is not None

    def sec(j, j0, cnt):
        return jnp.clip(j - j0, 0, cnt - 1)

    in_specs = [
        pl.BlockSpec((tm, d), lambda i, j: (i, 0)),
        pl.BlockSpec((1, d), lambda i, j: (0, 0)),
        pl.BlockSpec((d, SEC), lambda i, j: (0, j)),
    ]
    args = [x2d, norm_g.reshape(1, d), w]
    if has_logf:
        in_specs += [pl.BlockSpec((d, LANES), lambda i, j: (0, 0)),
                     pl.BlockSpec((1, LANES), lambda i, j: (0, 0))]
        args += [wf, bf]
    in_specs += [pl.BlockSpec((SEC, LANES), lambda i, j: (0, 0)),
                 pl.BlockSpec((SEC, LANES), lambda i, j: (0, 0))]
    args += [qgain, kgain]

    out_shapes, out_specs = [], []
    for kind, j0, cnt in layout:
        if kind == "f32" or kind == "k":
            dt = F32 if kind == "f32" else BF16
            out_shapes.append(jax.ShapeDtypeStruct((n_rows, cnt * SEC), dt))
            out_specs.append(pl.BlockSpec(
                (tm, SEC), lambda i, j, j0=j0, cnt=cnt: (i, sec(j, j0, cnt))))
        elif kind == "qT":
            out_shapes.append(jax.ShapeDtypeStruct((batch, cnt * SEC, seq), BF16))
            out_specs.append(pl.BlockSpec(
                (None, SEC, tm), lambda i, j, j0=j0, cnt=cnt: (i // nt, sec(j, j0, cnt), i % nt)))
        else:
            out_shapes.append(jax.ShapeDtypeStruct((batch, nt, cnt * SEC, tm), BF16))
            out_specs.append(pl.BlockSpec(
                (None, None, SEC, tm),
                lambda i, j, j0=j0, cnt=cnt: (i // nt, i % nt, sec(j, j0, cnt), 0)))
    if has_logf:
        out_shapes.append(jax.ShapeDtypeStruct((n_rows, LANES), F32))
        out_specs.append(pl.BlockSpec((tm, LANES), lambda i, j: (i, 0)))

    return pl.pallas_call(
        functools.partial(_proj_kernel, layout=layout, head_dim=head_dim, has_logf=has_logf),
        grid=(n_rows // tm, nsec),
        in_specs=in_specs,
        out_specs=out_specs,
        out_shape=out_shapes,
        scratch_shapes=[pltpu.VMEM((tm, d), BF16)],
        compiler_params=_params(("arbitrary", "arbitrary")),
        name="rms_in_proj",
    )(*args)


def _out_kernel(*refs, n_in):
    ys, ws = refs[:n_in], refs[n_in:2 * n_in]
    x_ref, o_ref = refs[2 * n_in], refs[2 * n_in + 1]
    acc = x_ref[...]
    for y_ref, w_ref in zip(ys, ws):
        acc = acc + jnp.dot(y_ref[...], w_ref[...], preferred_element_type=F32)
    o_ref[...] = acc


def _out_project(ys, ws, x2d, *, tm):
    n_rows, d = x2d.shape
    n_in = len(ys)
    in_specs = [pl.BlockSpec((tm, y.shape[1]), lambda i: (i, 0)) for y in ys]
    in_specs += [pl.BlockSpec(w.shape, lambda i: (0, 0)) for w in ws]
    in_specs += [pl.BlockSpec((tm, d), lambda i: (i, 0))]
    return pl.pallas_call(
        functools.partial(_out_kernel, n_in=n_in),
        grid=(n_rows // tm,),
        in_specs=in_specs,
        out_specs=pl.BlockSpec((tm, d), lambda i: (i, 0)),
        out_shape=jax.ShapeDtypeStruct((n_rows, d), F32),
        compiler_params=_params(("arbitrary",)),
        name="out_proj_residual",
    )(*ys, *ws, x2d)


def _s5_prepare(lam_re, lam_im, log_dt, b_re, b_im, c_re, c_im):
    dt = jnp.exp(log_dt.astype(F32))[:, None]
    lr, li = lam_re.astype(F32), lam_im.astype(F32)
    mag = jnp.exp(lr * dt)
    ar, ai = mag * jnp.cos(li * dt), mag * jnp.sin(li * dt)
    den = lr * lr + li * li
    nr, ni = ar - 1.0, ai
    kr = (nr * lr + ni * li) / den
    ki = (ni * lr - nr * li) / den
    br, bi = b_re.astype(F32), b_im.astype(F32)
    bbr = kr[..., None] * br - ki[..., None] * bi
    bbi = kr[..., None] * bi + ki[..., None] * br
    nblk = S5_GROUPS // 16
    eye = jnp.eye(16, dtype=F32)

    def b_tiles(bb):
        t = bb.reshape(nblk, 16, S5_SLICES, 8, S5_GROUP)
        return jnp.einsum("ngkph,fg->knfhgp", t, eye).reshape(S5_SLICES, nblk, 256, LANES)

    def c_tiles(cc):
        t = cc.reshape(nblk, 16, S5_GROUP, S5_SLICES, 8)
        return jnp.einsum("nghkp,fg->nkgpfh", t, eye).reshape(nblk, S5_SLICES, LANES, 256)

    bmat = jnp.concatenate([b_tiles(bbr), b_tiles(bbi)], axis=-1).astype(BF16)
    cmat = jnp.concatenate([c_tiles(c_re.astype(F32)), c_tiles(-c_im.astype(F32))], axis=2)
    cmat = cmat.reshape(nblk, S5_SLICES * 256, 256).astype(BF16)

    def a_tiles(a):
        t = a.reshape(nblk, 16, S5_SLICES, 8).transpose(2, 0, 1, 3)
        return t.reshape(S5_SLICES, nblk * LANES)

    return bmat, cmat, a_tiles(ar), a_tiles(ai)


def _s5_kernel(u_ref, z_ref, bm_ref, cm_ref, ar_ref, ai_ref, d_ref, wg_ref, bg_ref,
               o_ref, s_ref, h_ref, *, chunk, pitch):
    nb = u_ref.shape[0]
    nblk = ar_ref.shape[1] // LANES

    @pl.when(pl.program_id(0) == 0)
    def _():
        h_ref[...] = jnp.zeros_like(h_ref)

    for b in range(nb):
        ub = u_ref[b].astype(BF16)
        for k in range(S5_SLICES):
            for n in range(nblk):
                x = jnp.dot(ub[:, n * 256:(n + 1) * 256], bm_ref[k, n],
                            preferred_element_type=F32)
                s_ref[b, 2 * n, k * pitch:k * pitch + chunk, :] = x[:, :LANES]
                s_ref[b, 2 * n + 1, k * pitch:k * pitch + chunk, :] = x[:, LANES:]

    ar = [ar_ref[:, n * LANES:(n + 1) * LANES] for n in range(nblk)]
    ai = [ai_ref[:, n * LANES:(n + 1) * LANES] for n in range(nblk)]

    def step(t, hs):
        out = []
        for b in range(nb):
            for n in range(nblk):
                hr, hi = hs[2 * (b * nblk + n)], hs[2 * (b * nblk + n) + 1]
                rows = pl.ds(t, S5_SLICES, stride=pitch)
                xr = s_ref[b, 2 * n, rows, :]
                xi = s_ref[b, 2 * n + 1, rows, :]
                nr = ar[n] * hr - ai[n] * hi + xr
                ni = ar[n] * hi + ai[n] * hr + xi
                s_ref[b, 2 * n, rows, :] = nr
                s_ref[b, 2 * n + 1, rows, :] = ni
                out += [nr, ni]
        return tuple(out)

    h0 = tuple(h_ref[b, s] for b in range(nb) for s in range(2 * nblk))
    hs = lax.fori_loop(0, chunk, step, h0, unroll=4)
    for b in range(nb):
        for s in range(2 * nblk):
            h_ref[b, s] = hs[b * 2 * nblk + s]

    for b in range(nb):
        ys = []
        for n in range(nblk):
            acc = None
            for k in range(S5_SLICES):
                rows = slice(k * pitch, k * pitch + chunk)
                hk = jnp.concatenate([s_ref[b, 2 * n, rows, :], s_ref[b, 2 * n + 1, rows, :]],
                                     axis=1).astype(BF16)
                part = jnp.dot(hk, cm_ref[n, k * 256:(k + 1) * 256, :],
                               preferred_element_type=F32)
                acc = part if acc is None else acc + part
            ys.append(acc)
        y = jnp.concatenate(ys, axis=1) + d_ref[...] * u_ref[b]
        y = jax.nn.gelu(y)
        gate = jax.nn.sigmoid(
            jnp.dot(y.astype(BF16), wg_ref[...], preferred_element_type=F32) + bg_ref[...])
        o_ref[b] = (y * gate * jax.nn.silu(z_ref[b])).astype(BF16)


def _s5_mixer(u, z, bmat, cmat, a_re, a_im, d, w_glu, b_glu, *, chunk):
    batch, seq, width = u.shape
    pitch = chunk + 8
    nblk = width // 256
    const = lambda shape: pl.BlockSpec(shape, lambda c: (0,) * len(shape))
    return pl.pallas_call(
        functools.partial(_s5_kernel, chunk=chunk, pitch=pitch),
        grid=(seq // chunk,),
        in_specs=[
            pl.BlockSpec((batch, chunk, width), lambda c: (0, c, 0)),
            pl.BlockSpec((batch, chunk, width), lambda c: (0, c, 0)),
            const(bmat.shape), const(cmat.shape), const(a_re.shape), const(a_im.shape),
            const((1, width)), const(w_glu.shape), const((1, width)),
        ],
        out_specs=pl.BlockSpec((batch, chunk, width), lambda c: (0, c, 0)),
        out_shape=jax.ShapeDtypeStruct((batch, seq, width), BF16),
        scratch_shapes=[
            pltpu.VMEM((batch, 2 * nblk, S5_SLICES * pitch, LANES), F32),
            pltpu.VMEM((batch, 2 * nblk, S5_SLICES, LANES), F32),
        ],
        compiler_params=_params(("arbitrary",)),
        name="s5_mixer",
    )(u, z, bmat, cmat, a_re, a_im, d.reshape(1, width), w_glu, b_glu.reshape(1, width))


def _split3(x):
    hi = x.astype(BF16).astype(F32)
    r = x - hi
    mid = r.astype(BF16).astype(F32)
    return hi, mid, r - mid


def _select_rows(index, values):
    out = jnp.zeros(index.shape, F32)
    for i, v in enumerate(values):
        out = jnp.where(index == i, v, out)
    return out


class _Chain:
    def __init__(self, kaug_blk, vt_blk, qaug_ref, stats, c, mult, mask_off):
        self.kaug_blk, self.vt_blk, self.qaug_ref = kaug_blk, vt_blk, qaug_ref
        self.m_ref, self.acc_ref = stats
        self.lanes = slice(c * CHUNK, (c + 1) * CHUNK)
        self.mult, self.mask_off = mult, mask_off

    def scores(self):
        s = jnp.dot(self.kaug_blk, self.qaug_ref[:, self.lanes],
                    preferred_element_type=F32)
        if self.mask_off is not None:
            key = lax.broadcasted_iota(jnp.int32, s.shape, 0)
            qry = lax.broadcasted_iota(jnp.int32, s.shape, 1)
            s = jnp.where(key - qry <= self.mask_off, s, NEG)
        return s * self.mult

    def softmax(self, t):
        m_prev = self.m_ref[:, self.lanes]
        m_new = jnp.maximum(m_prev, jnp.max(t, axis=0, keepdims=True))
        alpha = jnp.exp2(m_prev - m_new)
        p = jnp.exp2(t - m_new)
        self.m_ref[:, self.lanes] = m_new
        return p.astype(BF16), alpha

    def values(self, p, alpha):
        pv = jnp.dot(self.vt_blk, p, preferred_element_type=F32)
        self.acc_ref[:, self.lanes] = alpha * self.acc_ref[:, self.lanes] + pv


def _run_chains(chains):
    ready = [ch.scores() for ch in chains[:SCORES_AHEAD]]
    for i, ch in enumerate(chains):
        if i + SCORES_AHEAD < len(chains):
            ready.append(chains[i + SCORES_AHEAD].scores())
        p, alpha = ch.softmax(ready.pop(0))
        ch.values(p, alpha)


def _causal_sweep(qi, tq, tk, load, make):
    per, nch = tq // tk, tq // CHUNK

    def body(j, carry):
        chains = []
        for d in range(per):
            blk = load(j * per + d)
            for c in range(nch):
                chains += make(blk, c, None)
        _run_chains(chains)
        return carry

    lax.fori_loop(0, qi, body, 0)
    chains = []
    for d in range(per):
        blk = load(qi * per + d)
        for c in range(nch):
            off = c * CHUNK - d * tk
            if off < -(CHUNK - 1):
                continue
            chains += make(blk, c, None if off >= tk - 1 else off)
    _run_chains(chains)


def _init_stats(*pairs):
    for m_ref, a_ref in pairs:
        m_ref[...] = jnp.full_like(m_ref, NEG)
        a_ref[...] = jnp.zeros_like(a_ref)


def _values_with_ones(vt_blk):
    row = lax.broadcasted_iota(jnp.int32, (BF16_ROWS, vt_blk.shape[1]), 0)
    ones = jnp.where(row == 0, 1.0, 0.0).astype(BF16)
    return jnp.concatenate([vt_blk, ones], axis=0)


def _normalised(acc_ref):
    return acc_ref[:HEAD_W, :] * (1.0 / acc_ref[HEAD_W:HEAD_W + 1, :])


def _fox_kernel(qt_ref, k_ref, vt_ref, c_ref, z_ref, o_ref,
                kaug_ref, qaug_ref, m_ref, acc_ref, *, tq, tk, scale):
    qi = pl.program_id(2)
    inv = 1.0 / scale

    @pl.when(qi == 0)
    def _():
        lane = lax.broadcasted_iota(jnp.int32, (LANES, LANES), 1)

        def fill(r, carry):
            rows = pl.ds(pl.multiple_of(r * LANES, LANES), LANES)
            col = jnp.broadcast_to(c_ref[pl.ds(r, 1), :], (LANES, LANES)).T
            hi, mid, lo = _split3(col * (-inv))
            ext = _select_rows(lane, (hi, mid, lo, 1.0, 1.0, 1.0))
            kaug_ref[rows, :LANES] = k_ref[rows, :]
            kaug_ref[rows, LANES:] = ext.astype(BF16)
            return carry

        lax.fori_loop(0, c_ref.shape[0], fill, 0)

    crow = c_ref[pl.ds(qi * (tq // LANES), 1), :]
    c0 = jnp.broadcast_to(crow[:, 0:1], (BF16_ROWS, tq)) * inv
    hi, mid, lo = _split3(c0)
    row = lax.broadcasted_iota(jnp.int32, (BF16_ROWS, tq), 0)
    qaug_ref[:HEAD_W, :] = qt_ref[...]
    qaug_ref[HEAD_W:HEAD_W + BF16_ROWS, :] = _select_rows(row, (1.0, 1.0, 1.0, hi, mid, lo)).astype(BF16)
    qaug_ref[HEAD_W + BF16_ROWS:, :] = jnp.zeros((AUG - HEAD_W - BF16_ROWS, tq), BF16)
    _init_stats((m_ref, acc_ref))

    def load(kb):
        ks = pl.multiple_of(kb * tk, tk)
        return kaug_ref[pl.ds(ks, tk), :], _values_with_ones(vt_ref[kb])

    def make(blk, c, mask_off):
        return [_Chain(blk[0], blk[1], qaug_ref, (m_ref, acc_ref), c, scale * LOG2E, mask_off)]

    _causal_sweep(qi, tq, tk, load, make)

    o = _normalised(acc_ref).T
    o_ref[...] = (o * jax.nn.silu(z_ref[...])).astype(BF16)


def _fox_attention(qt, k, vt, c2, z, *, tq, tk):
    batch, width, seq = qt.shape
    heads = width // HEAD_W
    scale = HEAD_W ** -0.5
    return pl.pallas_call(
        functools.partial(_fox_kernel, tq=tq, tk=tk, scale=scale),
        grid=(batch, heads, seq // tq),
        in_specs=[
            pl.BlockSpec((None, HEAD_W, tq), lambda b, h, q: (b, h, q)),
            pl.BlockSpec((None, seq, HEAD_W), lambda b, h, q: (b, 0, h)),
            pl.BlockSpec((None, seq // tk, HEAD_W, tk), lambda b, h, q: (b, 0, h, 0)),
            pl.BlockSpec((None, None, seq // LANES, LANES), lambda b, h, q: (b, h, 0, 0)),
            pl.BlockSpec((None, tq, HEAD_W), lambda b, h, q: (b, q, h)),
        ],
        out_specs=pl.BlockSpec((None, tq, HEAD_W), lambda b, h, q: (b, q, h)),
        out_shape=jax.ShapeDtypeStruct((batch, seq, width), BF16),
        scratch_shapes=[
            pltpu.VMEM((seq, AUG), BF16), pltpu.VMEM((AUG, tq), BF16),
            pltpu.VMEM((1, tq), F32), pltpu.VMEM((ACC_ROWS, tq), F32),
        ],
        compiler_params=_params(("arbitrary", "arbitrary", "arbitrary")),
        name="fox_attention",
    )(qt, k, vt, c2, z)


def _diff_kernel(qt_ref, k_ref, vt_ref, hp_ref, og_ref, z_ref, o_ref,
                 kaug_ref, qa0_ref, qa1_ref, m0_ref, a0_ref, m1_ref, a1_ref,
                 *, tq, tk, out_scale):
    qi = pl.program_id(2)
    dh = qt_ref.shape[0] // 2
    slope = hp_ref[0:1, :]
    lam = hp_ref[1:2, 0:1]

    @pl.when(qi == 0)
    def _():
        lane = lax.broadcasted_iota(jnp.int32, (LANES, LANES), 1)
        key = lax.broadcasted_iota(jnp.int32, (LANES, LANES), 0)

        def fill(r, carry):
            rows = pl.ds(pl.multiple_of(r * LANES, LANES), LANES)
            base = (key * 0 + r * LANES).astype(F32)
            ext = _select_rows(lane, (slope * base, slope * key.astype(F32), 1.0))
            kaug_ref[rows, :LANES] = k_ref[rows, :]
            kaug_ref[rows, LANES:] = ext.astype(BF16)
            return carry

        lax.fori_loop(0, k_ref.shape[0] // LANES, fill, 0)

    row = lax.broadcasted_iota(jnp.int32, (BF16_ROWS, tq), 0)
    q0 = (row * 0 + qi * tq).astype(F32)
    slope_t = jnp.concatenate([slope] * (tq // LANES), axis=1)
    ext = _select_rows(row, (1.0, 1.0, -slope_t * q0)).astype(BF16)
    qt = qt_ref[...]
    first = lax.broadcasted_iota(jnp.int32, qt.shape, 0) < dh
    zero = jnp.zeros_like(qt)
    for qa_ref, comp in ((qa0_ref, jnp.where(first, qt, zero)), (qa1_ref, jnp.where(first, zero, qt))):
        qa_ref[:HEAD_W, :] = comp
        qa_ref[HEAD_W:HEAD_W + BF16_ROWS, :] = ext
        qa_ref[HEAD_W + BF16_ROWS:, :] = jnp.zeros((AUG - HEAD_W - BF16_ROWS, tq), BF16)
    _init_stats((m0_ref, a0_ref), (m1_ref, a1_ref))

    def load(kb):
        ks = pl.multiple_of(kb * tk, tk)
        return kaug_ref[pl.ds(ks, tk), :], _values_with_ones(vt_ref[kb])

    def make(blk, c, mask_off):
        return [_Chain(blk[0], blk[1], qa0_ref, (m0_ref, a0_ref), c, LOG2E, mask_off),
                _Chain(blk[0], blk[1], qa1_ref, (m1_ref, a1_ref), c, LOG2E, mask_off)]

    _causal_sweep(qi, tq, tk, load, make)

    ot = _normalised(a0_ref) - lam * _normalised(a1_ref)
    ms = jnp.mean(ot * ot, axis=0, keepdims=True)
    o = (ot * lax.rsqrt(ms + EPS)).T
    o = o * og_ref[...] * out_scale
    o_ref[...] = (o * jax.nn.silu(z_ref[...])).astype(BF16)


def _diff_attention(qt, k, vt, head_params, out_gain, z, *, tq, tk, out_scale):
    batch, width, seq = qt.shape
    heads = width // HEAD_W
    stat = lambda: pltpu.VMEM((1, tq), F32)
    accum = lambda: pltpu.VMEM((ACC_ROWS, tq), F32)
    return pl.pallas_call(
        functools.partial(_diff_kernel, tq=tq, tk=tk, out_scale=out_scale),
        grid=(batch, heads, seq // tq),
        in_specs=[
            pl.BlockSpec((None, HEAD_W, tq), lambda b, h, q: (b, h, q)),
            pl.BlockSpec((None, seq, HEAD_W), lambda b, h, q: (b, 0, h)),
            pl.BlockSpec((None, seq // tk, HEAD_W, tk), lambda b, h, q: (b, 0, h, 0)),
            pl.BlockSpec((None, 8, LANES), lambda b, h, q: (h, 0, 0)),
            pl.BlockSpec((1, HEAD_W), lambda b, h, q: (0, 0)),
            pl.BlockSpec((None, tq, HEAD_W), lambda b, h, q: (b, q, h)),
        ],
        out_specs=pl.BlockSpec((None, tq, HEAD_W), lambda b, h, q: (b, q, h)),
        out_shape=jax.ShapeDtypeStruct((batch, seq, width), BF16),
        scratch_shapes=[pltpu.VMEM((seq, AUG), BF16),
                        pltpu.VMEM((AUG, tq), BF16), pltpu.VMEM((AUG, tq), BF16),
                        stat(), accum(), stat(), accum()],
        compiler_params=_params(("arbitrary", "arbitrary", "arbitrary")),
        name="diff_attention",
    )(qt, k, vt, head_params, out_gain, z)


def _cumsum_kernel(lf_ref, o_ref, carry_ref, *, rows):
    @pl.when(pl.program_id(1) == 0)
    def _():
        carry_ref[...] = jnp.zeros_like(carry_ref)

    x = lf_ref[...]
    r = lax.broadcasted_iota(jnp.int32, (rows, rows), 0)
    c = lax.broadcasted_iota(jnp.int32, (rows, rows), 1)
    tri = (c <= r).astype(BF16)
    hi = x.astype(BF16)
    r1 = x - hi.astype(F32)
    mid = r1.astype(BF16)
    lo = (r1 - mid.astype(F32)).astype(BF16)
    cum = (jnp.dot(tri, hi, preferred_element_type=F32)
           + jnp.dot(tri, mid, preferred_element_type=F32)
           + jnp.dot(tri, lo, preferred_element_type=F32)) + carry_ref[...]
    carry_ref[...] = cum[rows - 1:rows, :]
    o_ref[...] = cum.T


def _cumsum_t(logf, *, rows):
    batch, seq, _ = logf.shape
    return pl.pallas_call(
        functools.partial(_cumsum_kernel, rows=rows),
        grid=(batch, seq // rows),
        in_specs=[pl.BlockSpec((None, rows, LANES), lambda b, i: (b, i, 0))],
        out_specs=pl.BlockSpec((None, LANES, rows), lambda b, i: (b, 0, i)),
        out_shape=jax.ShapeDtypeStruct((batch, LANES, seq), F32),
        scratch_shapes=[pltpu.VMEM((1, LANES), F32)],
        compiler_params=_params(("arbitrary", "arbitrary")),
        name="logf_cumsum",
    )(logf)


def _lane_col(v):
    return jnp.broadcast_to(v.astype(F32)[:, None], (v.shape[0], LANES))


def _alibi_slopes(n):
    return jnp.asarray([2.0 ** (-8.0 * (h + 1) / n) for h in range(n)], F32)


def _even_layer(x2d, layer, batch, seq, tiles, e_norm, e_w_in, e_w_out, lam_re, lam_im, log_dt,
                b_re, b_im, c_re, c_im, d, w_glu, b_glu, q_norm, k_norm, lq1, lk1, lq2, lk2,
                out_norm):
    tm, tq, tk, chunk = tiles
    s5w = S5_GROUPS * S5_GROUP
    lambda_init = 0.8 - 0.6 * math.exp(-0.3 * layer)
    qgain = _lane_col(jnp.tile(q_norm.astype(F32) * (DA_HEAD_DIM ** -0.5), DA_HEADS))
    kgain = _lane_col(jnp.tile(k_norm, DA_HEADS))
    layout = (("f32", 0, 1), ("f32", 1, 1), ("qT", 2, 1), ("k", 3, 1), ("vT", 4, 1), ("f32", 5, 1))
    u, zs, qt, k, vt, zd = _project(
        x2d, e_norm, e_w_in.astype(BF16), qgain, kgain, batch=batch, seq=seq, tm=tm,
        layout=layout, head_dim=DA_HEAD_DIM)

    bmat, cmat, a_re, a_im = _s5_prepare(lam_re, lam_im, log_dt, b_re, b_im, c_re, c_im)
    y_s5 = _s5_mixer(u.reshape(batch, seq, s5w), zs.reshape(batch, seq, s5w), bmat, cmat,
                     a_re, a_im, d, w_glu.astype(BF16), b_glu, chunk=chunk)

    lam = (jnp.exp(jnp.sum(lq1.astype(F32) * lk1.astype(F32)))
           - jnp.exp(jnp.sum(lq2.astype(F32) * lk2.astype(F32))) + lambda_init)
    head_params = jnp.zeros((DA_HEADS, 8, LANES), F32)
    head_params = head_params.at[:, 0, :].set(_alibi_slopes(DA_HEADS)[:, None])
    head_params = head_params.at[:, 1, :].set(lam)
    y_da = _diff_attention(qt, k.reshape(batch, seq, -1), vt, head_params,
                           out_norm.astype(F32).reshape(1, HEAD_W),
                           zd.reshape(batch, seq, -1), tq=tq, tk=tk,
                           out_scale=1.0 - lambda_init)

    w_out = e_w_out.astype(BF16)
    return _out_project([y_s5.reshape(batch * seq, s5w), y_da.reshape(batch * seq, -1)],
                        [w_out[:s5w], w_out[s5w:]], x2d, tm=tm)


def _odd_layer(x2d, batch, seq, tiles, o_norm, o_w_in, o_b_f, o_w_out, q_norm, k_norm):
    tm, tq, tk, _ = tiles
    width = FOX_HEADS * HEAD_W
    nsec = width // SEC
    qgain = _lane_col(jnp.tile(q_norm, SEC // HEAD_W))
    kgain = _lane_col(jnp.tile(k_norm, SEC // HEAD_W))
    w = o_w_in.astype(BF16)
    wf = jnp.zeros((w.shape[0], LANES), BF16).at[:, :FOX_HEADS].set(w[:, 4 * width:])
    bf = jnp.zeros((1, LANES), F32).at[0, :FOX_HEADS].set(o_b_f.astype(F32))
    layout = (("qT", 0, nsec), ("k", nsec, nsec), ("vT", 2 * nsec, nsec), ("f32", 3 * nsec, nsec))
    qt, k, vt, z, logf = _project(
        x2d, o_norm, w[:, :4 * width], qgain, kgain, batch=batch, seq=seq, tm=tm,
        layout=layout, head_dim=HEAD_W, wf=wf, bf=bf)
    ct = _cumsum_t(logf.reshape(batch, seq, LANES), rows=tm)
    c2 = ct[:, :FOX_HEADS].reshape(batch, FOX_HEADS, seq // LANES, LANES)
    y = _fox_attention(qt, k.reshape(batch, seq, width), vt, c2, z.reshape(batch, seq, width),
                       tq=tq, tk=tk)
    return _out_project([y.reshape(batch * seq, width)], [o_w_out.astype(BF16)], x2d, tm=tm)


def _tiles(seq):
    tm = min(512, seq)
    return tm, min(2048, seq), tm, min(128, seq)


def kernel(x, e_norm, e_w_in, e_w_out, s5_lambda_re, s5_lambda_im, s5_log_dt, s5_b_re, s5_b_im, s5_c_re, s5_c_im, s5_d, s5_w_glu, s5_b_glu, da_q_norm, da_k_norm, da_lambda_q1, da_lambda_k1, da_lambda_q2, da_lambda_k2, da_out_norm, o_norm, o_w_in, o_b_f, o_w_out, fox_q_norm, fox_k_norm):
    batch, seq, d_model = x.shape
    tiles = _tiles(seq)
    depth = e_norm.shape[0] + o_norm.shape[0]
    x2d = x.reshape(batch * seq, d_model)
    for layer in range(depth):
        j = layer // 2
        if layer % 2 == 0:
            x2d = _even_layer(
                x2d, layer, batch, seq, tiles, e_norm[j], e_w_in[j], e_w_out[j],
                s5_lambda_re[j], s5_lambda_im[j], s5_log_dt[j], s5_b_re[j], s5_b_im[j],
                s5_c_re[j], s5_c_im[j], s5_d[j], s5_w_glu[j], s5_b_glu[j],
                da_q_norm[j], da_k_norm[j], da_lambda_q1[j], da_lambda_k1[j],
                da_lambda_q2[j], da_lambda_k2[j], da_out_norm[j])
        else:
            x2d = _odd_layer(x2d, batch, seq, tiles, o_norm[j], o_w_in[j], o_b_f[j], o_w_out[j],
                             fox_q_norm[j], fox_k_norm[j])
    return x2d.reshape(batch, seq, d_model)
```

```python
import functools
import math

import jax
import jax.numpy as jnp
from jax import lax
from jax.experimental import pallas as pl
from jax.experimental.pallas import tpu as pltpu

F32 = jnp.float32
BF16 = jnp.bfloat16
EPS = 1e-6
NEG = -1e30
LOG2E = math.log2(math.e)

LANES = 128
BF16_ROWS = 16
AUG = 256
CHUNK = 256
SCORES_AHEAD = 3
SEC = 1024
VMEM_LIMIT = 52 * 1024 * 1024

S5_GROUPS = 64
S5_GROUP = 16
S5_STATE = 64
S5_SLICES = 8
DA_HEADS = 8
DA_HEAD_DIM = 64
FOX_HEADS = 16
HEAD_W = 128
ACC_ROWS = HEAD_W + BF16_ROWS


def _params(sem):
    return pltpu.CompilerParams(dimension_semantics=sem, vmem_limit_bytes=VMEM_LIMIT)


def _head_norm_t(acc_t, gain_ref, head_dim):
    tm = acc_t.shape[1]
    r = acc_t.reshape(SEC // head_dim, head_dim, tm)
    ms = jnp.mean(r * r, axis=1, keepdims=True)
    n = (r * lax.rsqrt(ms + EPS)).reshape(SEC, tm)
    g = gain_ref[...]
    return jnp.concatenate(
        [n[:, c * LANES:(c + 1) * LANES] * g for c in range(tm // LANES)], axis=1)


def _proj_kernel(*refs, layout, head_dim, has_logf):
    it = iter(refs)
    x_ref, g_ref, w_ref = next(it), next(it), next(it)
    if has_logf:
        wf_ref, bf_ref = next(it), next(it)
    qg_ref, kg_ref = next(it), next(it)
    out_refs = [next(it) for _ in layout]
    if has_logf:
        logf_ref = next(it)
    hs_ref = next(it)

    j = pl.program_id(1)

    @pl.when(j == 0)
    def _():
        x = x_ref[...]
        ms = jnp.mean(x * x, axis=-1, keepdims=True)
        hs_ref[...] = (x * lax.rsqrt(ms + EPS) * g_ref[...]).astype(BF16)
        if has_logf:
            f = jnp.dot(hs_ref[...], wf_ref[...], preferred_element_type=F32) + bf_ref[...]
            logf_ref[...] = jax.nn.log_sigmoid(f)

    acc = jnp.dot(hs_ref[...], w_ref[...], preferred_element_type=F32)

    for (kind, j0, cnt), o_ref in zip(layout, out_refs):
        @pl.when((j >= j0) & (j < j0 + cnt))
        def _(kind=kind, o_ref=o_ref):
            if kind == "f32":
                o_ref[...] = acc
            elif kind == "vT":
                o_ref[...] = acc.T.astype(BF16)
            elif kind == "qT":
                o_ref[...] = _head_norm_t(acc.T, qg_ref, head_dim).astype(BF16)
            else:
                o_ref[...] = _head_norm_t(acc.T, kg_ref, head_dim).T.astype(BF16)


def _project(x2d, norm_g, w, qgain, kgain, *, batch, seq, tm, layout, head_dim, wf=None, bf=None):
    n_rows, d = x2d.shape
    nsec = w.shape[1] // SEC
    nt = seq // tm
    has_logf = wf is not None

    def sec(j, j0, cnt):
        return jnp.clip(j - j0, 0, cnt - 1)

    in_specs = [
        pl.BlockSpec((tm, d), lambda i, j: (i, 0)),
        pl.BlockSpec((1, d), lambda i, j: (0, 0)),
        pl.BlockSpec((d, SEC), lambda i, j: (0, j)),
    ]
    args = [x2d, norm_g.reshape(1, d), w]
    if has_logf:
        in_specs += [pl.BlockSpec((d, LANES), lambda i, j: (0, 0)),
                     pl.BlockSpec((1, LANES), lambda i, j: (0, 0))]
        args += [wf, bf]
    in_specs += [pl.BlockSpec((SEC, LANES), lambda i, j: (0, 0)),
                 pl.BlockSpec((SEC, LANES), lambda i, j: (0, 0))]
    args += [qgain, kgain]

    out_shapes, out_specs = [], []
    for kind, j0, cnt in layout:
        if kind == "f32" or kind == "k":
            dt = F32 if kind == "f32" else BF16
            out_shapes.append(jax.ShapeDtypeStruct((n_rows, cnt * SEC), dt))
            out_specs.append(pl.BlockSpec(
                (tm, SEC), lambda i, j, j0=j0, cnt=cnt: (i, sec(j, j0, cnt))))
        elif kind == "qT":
            out_shapes.append(jax.ShapeDtypeStruct((batch, cnt * SEC, seq), BF16))
            out_specs.append(pl.BlockSpec(
                (None, SEC, tm), lambda i, j, j0=j0, cnt=cnt: (i // nt, sec(j, j0, cnt), i % nt)))
        else:
            out_shapes.append(jax.ShapeDtypeStruct((batch, nt, cnt * SEC, tm), BF16))
            out_specs.append(pl.BlockSpec(
                (None, None, SEC, tm),
                lambda i, j, j0=j0, cnt=cnt: (i // nt, i % nt, sec(j, j0, cnt), 0)))
    if has_logf:
        out_shapes.append(jax.ShapeDtypeStruct((n_rows, LANES), F32))
        out_specs.append(pl.BlockSpec((tm, LANES), lambda i, j: (i, 0)))

    return pl.pallas_call(
        functools.partial(_proj_kernel, layout=layout, head_dim=head_dim, has_logf=has_logf),
        grid=(n_rows // tm, nsec),
        in_specs=in_specs,
        out_specs=out_specs,
        out_shape=out_shapes,
        scratch_shapes=[pltpu.VMEM((tm, d), BF16)],
        compiler_params=_params(("arbitrary", "arbitrary")),
        name="rms_in_proj",
    )(*args)


def _out_kernel(*refs, n_in):
    ys, ws = refs[:n_in], refs[n_in:2 * n_in]
    x_ref, o_ref = refs[2 * n_in], refs[2 * n_in + 1]
    acc = x_ref[...]
    for y_ref, w_ref in zip(ys, ws):
        acc = acc + jnp.dot(y_ref[...], w_ref[...], preferred_element_type=F32)
    o_ref[...] = acc


def _out_project(ys, ws, x2d, *, tm):
    n_rows, d = x2d.shape
    n_in = len(ys)
    in_specs = [pl.BlockSpec((tm, y.shape[1]), lambda i: (i, 0)) for y in ys]
    in_specs += [pl.BlockSpec(w.shape, lambda i: (0, 0)) for w in ws]
    in_specs += [pl.BlockSpec((tm, d), lambda i: (i, 0))]
    return pl.pallas_call(
        functools.partial(_out_kernel, n_in=n_in),
        grid=(n_rows // tm,),
        in_specs=in_specs,
        out_specs=pl.BlockSpec((tm, d), lambda i: (i, 0)),
        out_shape=jax.ShapeDtypeStruct((n_rows, d), F32),
        compiler_params=_params(("arbitrary",)),
        name="out_proj_residual",
    )(*ys, *ws, x2d)


def _s5_prepare(lam_re, lam_im, log_dt, b_re, b_im, c_re, c_im):
    dt = jnp.exp(log_dt.astype(F32))[:, None]
    lr, li = lam_re.astype(F32), lam_im.astype(F32)
    mag = jnp.exp(lr * dt)
    ar, ai = mag * jnp.cos(li * dt), mag * jnp.sin(li * dt)
    den = lr * lr + li * li
    nr, ni = ar - 1.0, ai
    kr = (nr * lr + ni * li) / den
    ki = (ni * lr - nr * li) / den
    br, bi = b_re.astype(F32), b_im.astype(F32)
    bbr = kr[..., None] * br - ki[..., None] * bi
    bbi = kr[..., None] * bi + ki[..., None] * br
    nblk = S5_GROUPS // 16
    eye = jnp.eye(16, dtype=F32)

    def b_tiles(bb):
        t = bb.reshape(nblk, 16, S5_SLICES, 8, S5_GROUP)
        return jnp.einsum("ngkph,fg->knfhgp", t, eye).reshape(S5_SLICES, nblk, 256, LANES)

    def c_tiles(cc):
        t = cc.reshape(nblk, 16, S5_GROUP, S5_SLICES, 8)
        return jnp.einsum("nghkp,fg->nkgpfh", t, eye).reshape(nblk, S5_SLICES, LANES, 256)

    bmat = jnp.concatenate([b_tiles(bbr), b_tiles(bbi)], axis=-1).astype(BF16)
    cmat = jnp.concatenate([c_tiles(c_re.astype(F32)), c_tiles(-c_im.astype(F32))], axis=2)
    cmat = cmat.reshape(nblk, S5_SLICES * 256, 256).astype(BF16)

    def a_tiles(a):
        t = a.reshape(nblk, 16, S5_SLICES, 8).transpose(2, 0, 1, 3)
        return t.reshape(S5_SLICES, nblk * LANES)

    return bmat, cmat, a_tiles(ar), a_tiles(ai)


def _s5_kernel(u_ref, z_ref, bm_ref, cm_ref, ar_ref, ai_ref, d_ref, wg_ref, bg_ref,
               o_ref, s_ref, h_ref, *, chunk, pitch):
    nb = u_ref.shape[0]
    nblk = ar_ref.shape[1] // LANES

    @pl.when(pl.program_id(0) == 0)
    def _():
        h_ref[...] = jnp.zeros_like(h_ref)

    for b in range(nb):
        ub = u_ref[b].astype(BF16)
        for k in range(S5_SLICES):
            for n in range(nblk):
                x = jnp.dot(ub[:, n * 256:(n + 1) * 256], bm_ref[k, n],
                            preferred_element_type=F32)
                s_ref[b, 2 * n, k * pitch:k * pitch + chunk, :] = x[:, :LANES]
                s_ref[b, 2 * n + 1, k * pitch:k * pitch + chunk, :] = x[:, LANES:]

    ar = [ar_ref[:, n * LANES:(n + 1) * LANES] for n in range(nblk)]
    ai = [ai_ref[:, n * LANES:(n + 1) * LANES] for n in range(nblk)]

    def step(t, hs):
        out = []
        for b in range(nb):
            for n in range(nblk):
                hr, hi = hs[2 * (b * nblk + n)], hs[2 * (b * nblk + n) + 1]
                rows = pl.ds(t, S5_SLICES, stride=pitch)
                xr = s_ref[b, 2 * n, rows, :]
                xi = s_ref[b, 2 * n + 1, rows, :]
                nr = ar[n] * hr - ai[n] * hi + xr
                ni = ar[n] * hi + ai[n] * hr + xi
                s_ref[b, 2 * n, rows, :] = nr
                s_ref[b, 2 * n + 1, rows, :] = ni
                out += [nr, ni]
        return tuple(out)

    h0 = tuple(h_ref[b, s] for b in range(nb) for s in range(2 * nblk))
    hs = lax.fori_loop(0, chunk, step, h0, unroll=4)
    for b in range(nb):
        for s in range(2 * nblk):
            h_ref[b, s] = hs[b * 2 * nblk + s]

    for b in range(nb):
        ys = []
        for n in range(nblk):
            acc = None
            for k in range(S5_SLICES):
                rows = slice(k * pitch, k * pitch + chunk)
                hk = jnp.concatenate([s_ref[b, 2 * n, rows, :], s_ref[b, 2 * n + 1, rows, :]],
                                     axis=1).astype(BF16)
                part = jnp.dot(hk, cm_ref[n, k * 256:(k + 1) * 256, :],
                               preferred_element_type=F32)
                acc = part if acc is None else acc + part
            ys.append(acc)
        y = jnp.concatenate(ys, axis=1) + d_ref[...] * u_ref[b]
        y = jax.nn.gelu(y)
        gate = jax.nn.sigmoid(
            jnp.dot(y.astype(BF16), wg_ref[...], preferred_element_type=F32) + bg_ref[...])
        o_ref[b] = (y * gate * jax.nn.silu(z_ref[b])).astype(BF16)


def _s5_mixer(u, z, bmat, cmat, a_re, a_im, d, w_glu, b_glu, *, chunk):
    batch, seq, width = u.shape
    pitch = chunk + 8
    nblk = width // 256
    const = lambda shape: pl.BlockSpec(shape, lambda c: (0,) * len(shape))
    return pl.pallas_call(
        functools.partial(_s5_kernel, chunk=chunk, pitch=pitch),
        grid=(seq // chunk,),
        in_specs=[
            pl.BlockSpec((batch, chunk, width), lambda c: (0, c, 0)),
            pl.BlockSpec((batch, chunk, width), lambda c: (0, c, 0)),
            const(bmat.shape), const(cmat.shape), const(a_re.shape), const(a_im.shape),
            const((1, width)), const(w_glu.shape), const((1, width)),
        ],
        out_specs=pl.BlockSpec((batch, chunk, width), lambda c: (0, c, 0)),
        out_shape=jax.ShapeDtypeStruct((batch, seq, width), BF16),
        scratch_shapes=[
            pltpu.VMEM((batch, 2 * nblk, S5_SLICES * pitch, LANES), F32),
            pltpu.VMEM((batch, 2 * nblk, S5_SLICES, LANES), F32),
        ],
        compiler_params=_params(("arbitrary",)),
        name="s5_mixer",
    )(u, z, bmat, cmat, a_re, a_im, d.reshape(1, width), w_glu, b_glu.reshape(1, width))


def _split3(x):
    hi = x.astype(BF16).astype(F32)
    r = x - hi
    mid = r.astype(BF16).astype(F32)
    return hi, mid, r - mid


def _select_rows(index, values):
    out = jnp.zeros(index.shape, F32)
    for i, v in enumerate(values):
        out = jnp.where(index == i, v, out)
    return out


class _Chain:
    def __init__(self, kaug_blk, vt_blk, qaug_ref, stats, c, mask_off):
        self.kaug_blk, self.vt_blk, self.qaug_ref = kaug_blk, vt_blk, qaug_ref
        self.m_ref, self.acc_ref = stats
        self.lanes = slice(c * CHUNK, (c + 1) * CHUNK)
        self.mask_off = mask_off

    def scores(self):
        s = jnp.dot(self.kaug_blk, self.qaug_ref[:, self.lanes],
                    preferred_element_type=F32)
        if self.mask_off is not None:
            key = lax.broadcasted_iota(jnp.int32, s.shape, 0)
            qry = lax.broadcasted_iota(jnp.int32, s.shape, 1)
            s = jnp.where(key - qry <= self.mask_off, s, NEG)
        return s

    def softmax(self, t):
        m_prev = self.m_ref[:, self.lanes]
        m_new = jnp.maximum(m_prev, jnp.max(t, axis=0, keepdims=True))
        alpha = jnp.exp2(m_prev - m_new)
        p = jnp.exp2(t - m_new)
        self.m_ref[:, self.lanes] = m_new
        return p.astype(BF16), alpha

    def values(self, p, alpha):
        pv = jnp.dot(self.vt_blk, p, preferred_element_type=F32)
        self.acc_ref[:, self.lanes] = alpha * self.acc_ref[:, self.lanes] + pv


def _run_chains(chains):
    ready = [ch.scores() for ch in chains[:SCORES_AHEAD]]
    for i, ch in enumerate(chains):
        if i + SCORES_AHEAD < len(chains):
            ready.append(chains[i + SCORES_AHEAD].scores())
        p, alpha = ch.softmax(ready.pop(0))
        ch.values(p, alpha)


def _causal_sweep(qi, tq, tk, load, make):
    per, nch = tq // tk, tq // CHUNK

    def body(j, carry):
        chains = []
        for d in range(per):
            blk = load(j * per + d)
            for c in range(nch):
                chains += make(blk, c, None)
        _run_chains(chains)
        return carry

    lax.fori_loop(0, qi, body, 0)
    chains = []
    for d in range(per):
        blk = load(qi * per + d)
        for c in range(nch):
            off = c * CHUNK - d * tk
            if off < -(CHUNK - 1):
                continue
            chains += make(blk, c, None if off >= tk - 1 else off)
    _run_chains(chains)


def _init_stats(*pairs):
    for m_ref, a_ref in pairs:
        m_ref[...] = jnp.full_like(m_ref, NEG)
        a_ref[...] = jnp.zeros_like(a_ref)


def _values_with_ones(vt_blk):
    row = lax.broadcasted_iota(jnp.int32, (BF16_ROWS, vt_blk.shape[1]), 0)
    ones = jnp.where(row == 0, 1.0, 0.0).astype(BF16)
    return jnp.concatenate([vt_blk, ones], axis=0)


def _normalised(acc_ref):
    return acc_ref[:HEAD_W, :] * (1.0 / acc_ref[HEAD_W:HEAD_W + 1, :])


def _fox_kernel(qt_ref, k_ref, vt_ref, c_ref, z_ref, o_ref,
                kaug_ref, qaug_ref, m_ref, acc_ref, *, tq, tk):
    qi = pl.program_id(2)

    @pl.when(qi == 0)
    def _():
        lane = lax.broadcasted_iota(jnp.int32, (LANES, LANES), 1)

        def fill(r, carry):
            rows = pl.ds(pl.multiple_of(r * LANES, LANES), LANES)
            col = jnp.broadcast_to(c_ref[pl.ds(r, 1), :], (LANES, LANES)).T
            hi, mid, lo = _split3(col * (-LOG2E))
            ext = _select_rows(lane, (hi, mid, lo, 1.0, 1.0, 1.0))
            kaug_ref[rows, :LANES] = k_ref[rows, :]
            kaug_ref[rows, LANES:] = ext.astype(BF16)
            return carry

        lax.fori_loop(0, c_ref.shape[0], fill, 0)

    crow = c_ref[pl.ds(qi * (tq // LANES), 1), :]
    c0 = jnp.broadcast_to(crow[:, 0:1], (BF16_ROWS, tq)) * LOG2E
    hi, mid, lo = _split3(c0)
    row = lax.broadcasted_iota(jnp.int32, (BF16_ROWS, tq), 0)
    qaug_ref[:HEAD_W, :] = qt_ref[...]
    qaug_ref[HEAD_W:HEAD_W + BF16_ROWS, :] = _select_rows(row, (1.0, 1.0, 1.0, hi, mid, lo)).astype(BF16)
    qaug_ref[HEAD_W + BF16_ROWS:, :] = jnp.zeros((AUG - HEAD_W - BF16_ROWS, tq), BF16)
    _init_stats((m_ref, acc_ref))

    def load(kb):
        ks = pl.multiple_of(kb * tk, tk)
        return kaug_ref[pl.ds(ks, tk), :], _values_with_ones(vt_ref[kb])

    def make(blk, c, mask_off):
        return [_Chain(blk[0], blk[1], qaug_ref, (m_ref, acc_ref), c, mask_off)]

    _causal_sweep(qi, tq, tk, load, make)

    o = _normalised(acc_ref).T
    o_ref[...] = (o * jax.nn.silu(z_ref[...])).astype(BF16)


def _fox_attention(qt, k, vt, c2, z, *, tq, tk):
    batch, width, seq = qt.shape
    heads = width // HEAD_W
    return pl.pallas_call(
        functools.partial(_fox_kernel, tq=tq, tk=tk),
        grid=(batch, heads, seq // tq),
        in_specs=[
            pl.BlockSpec((None, HEAD_W, tq), lambda b, h, q: (b, h, q)),
            pl.BlockSpec((None, seq, HEAD_W), lambda b, h, q: (b, 0, h)),
            pl.BlockSpec((None, seq // tk, HEAD_W, tk), lambda b, h, q: (b, 0, h, 0)),
            pl.BlockSpec((None, None, seq // LANES, LANES), lambda b, h, q: (b, h, 0, 0)),
            pl.BlockSpec((None, tq, HEAD_W), lambda b, h, q: (b, q, h)),
        ],
        out_specs=pl.BlockSpec((None, tq, HEAD_W), lambda b, h, q: (b, q, h)),
        out_shape=jax.ShapeDtypeStruct((batch, seq, width), BF16),
        scratch_shapes=[
            pltpu.VMEM((seq, AUG), BF16), pltpu.VMEM((AUG, tq), BF16),
            pltpu.VMEM((1, tq), F32), pltpu.VMEM((ACC_ROWS, tq), F32),
        ],
        compiler_params=_params(("arbitrary", "arbitrary", "arbitrary")),
        name="fox_attention",
    )(qt, k, vt, c2, z)


def _diff_kernel(qt_ref, k_ref, vt_ref, hp_ref, og_ref, z_ref, o_ref,
                 kaug_ref, qa0_ref, qa1_ref, m0_ref, a0_ref, m1_ref, a1_ref,
                 *, tq, tk, out_scale):
    qi = pl.program_id(2)
    dh = qt_ref.shape[0] // 2
    slope2 = hp_ref[0:1, :] * LOG2E
    lam = hp_ref[1:2, 0:1]

    @pl.when(qi == 0)
    def _():
        lane = lax.broadcasted_iota(jnp.int32, (LANES, LANES), 1)
        key = lax.broadcasted_iota(jnp.int32, (LANES, LANES), 0)

        def fill(r, carry):
            rows = pl.ds(pl.multiple_of(r * LANES, LANES), LANES)
            base = (key * 0 + r * LANES).astype(F32)
            ext = _select_rows(lane, _split3(slope2 * base) + _split3(slope2 * key.astype(F32))
                               + (1.0, 1.0, 1.0))
            kaug_ref[rows, :LANES] = k_ref[rows, :]
            kaug_ref[rows, LANES:] = ext.astype(BF16)
            return carry

        lax.fori_loop(0, k_ref.shape[0] // LANES, fill, 0)

    row = lax.broadcasted_iota(jnp.int32, (BF16_ROWS, tq), 0)
    q0 = (row * 0 + qi * tq).astype(F32)
    slope_t = jnp.concatenate([slope2] * (tq // LANES), axis=1)
    ext = _select_rows(row, (1.0,) * 6 + _split3(-slope_t * q0)).astype(BF16)
    qt = qt_ref[...]
    first = lax.broadcasted_iota(jnp.int32, qt.shape, 0) < dh
    zero = jnp.zeros_like(qt)
    for qa_ref, comp in ((qa0_ref, jnp.where(first, qt, zero)), (qa1_ref, jnp.where(first, zero, qt))):
        qa_ref[:HEAD_W, :] = comp
        qa_ref[HEAD_W:HEAD_W + BF16_ROWS, :] = ext
        qa_ref[HEAD_W + BF16_ROWS:, :] = jnp.zeros((AUG - HEAD_W - BF16_ROWS, tq), BF16)
    _init_stats((m0_ref, a0_ref), (m1_ref, a1_ref))

    def load(kb):
        ks = pl.multiple_of(kb * tk, tk)
        return kaug_ref[pl.ds(ks, tk), :], _values_with_ones(vt_ref[kb])

    def make(blk, c, mask_off):
        return [_Chain(blk[0], blk[1], qa0_ref, (m0_ref, a0_ref), c, mask_off),
                _Chain(blk[0], blk[1], qa1_ref, (m1_ref, a1_ref), c, mask_off)]

    _causal_sweep(qi, tq, tk, load, make)

    ot = _normalised(a0_ref) - lam * _normalised(a1_ref)
    ms = jnp.mean(ot * ot, axis=0, keepdims=True)
    o = (ot * lax.rsqrt(ms + EPS)).T
    o = o * og_ref[...] * out_scale
    o_ref[...] = (o * jax.nn.silu(z_ref[...])).astype(BF16)


def _diff_attention(qt, k, vt, head_params, out_gain, z, *, tq, tk, out_scale):
    batch, width, seq = qt.shape
    heads = width // HEAD_W
    stat = lambda: pltpu.VMEM((1, tq), F32)
    accum = lambda: pltpu.VMEM((ACC_ROWS, tq), F32)
    return pl.pallas_call(
        functools.partial(_diff_kernel, tq=tq, tk=tk, out_scale=out_scale),
        grid=(batch, heads, seq // tq),
        in_specs=[
            pl.BlockSpec((None, HEAD_W, tq), lambda b, h, q: (b, h, q)),
            pl.BlockSpec((None, seq, HEAD_W), lambda b, h, q: (b, 0, h)),
            pl.BlockSpec((None, seq // tk, HEAD_W, tk), lambda b, h, q: (b, 0, h, 0)),
            pl.BlockSpec((None, 8, LANES), lambda b, h, q: (h, 0, 0)),
            pl.BlockSpec((1, HEAD_W), lambda b, h, q: (0, 0)),
            pl.BlockSpec((None, tq, HEAD_W), lambda b, h, q: (b, q, h)),
        ],
        out_specs=pl.BlockSpec((None, tq, HEAD_W), lambda b, h, q: (b, q, h)),
        out_shape=jax.ShapeDtypeStruct((batch, seq, width), BF16),
        scratch_shapes=[pltpu.VMEM((seq, AUG), BF16),
                        pltpu.VMEM((AUG, tq), BF16), pltpu.VMEM((AUG, tq), BF16),
                        stat(), accum(), stat(), accum()],
        compiler_params=_params(("arbitrary", "arbitrary", "arbitrary")),
        name="diff_attention",
    )(qt, k, vt, head_params, out_gain, z)


def _cumsum_kernel(lf_ref, o_ref, carry_ref, *, rows):
    @pl.when(pl.program_id(1) == 0)
    def _():
        carry_ref[...] = jnp.zeros_like(carry_ref)

    x = lf_ref[...]
    r = lax.broadcasted_iota(jnp.int32, (rows, rows), 0)
    c = lax.broadcasted_iota(jnp.int32, (rows, rows), 1)
    tri = (c <= r).astype(BF16)
    hi = x.astype(BF16)
    r1 = x - hi.astype(F32)
    mid = r1.astype(BF16)
    lo = (r1 - mid.astype(F32)).astype(BF16)
    cum = (jnp.dot(tri, hi, preferred_element_type=F32)
           + jnp.dot(tri, mid, preferred_element_type=F32)
           + jnp.dot(tri, lo, preferred_element_type=F32)) + carry_ref[...]
    carry_ref[...] = cum[rows - 1:rows, :]
    o_ref[...] = cum.T


def _cumsum_t(logf, *, rows):
    batch, seq, _ = logf.shape
    return pl.pallas_call(
        functools.partial(_cumsum_kernel, rows=rows),
        grid=(batch, seq // rows),
        in_specs=[pl.BlockSpec((None, rows, LANES), lambda b, i: (b, i, 0))],
        out_specs=pl.BlockSpec((None, LANES, rows), lambda b, i: (b, 0, i)),
        out_shape=jax.ShapeDtypeStruct((batch, LANES, seq), F32),
        scratch_shapes=[pltpu.VMEM((1, LANES), F32)],
        compiler_params=_params(("arbitrary", "arbitrary")),
        name="logf_cumsum",
    )(logf)


def _lane_col(v):
    return jnp.broadcast_to(v.astype(F32)[:, None], (v.shape[0], LANES))


def _alibi_slopes(n):
    return jnp.asarray([2.0 ** (-8.0 * (h + 1) / n) for h in range(n)], F32)


def _even_layer(x2d, layer, batch, seq, tiles, e_norm, e_w_in, e_w_out, lam_re, lam_im, log_dt,
                b_re, b_im, c_re, c_im, d, w_glu, b_glu, q_norm, k_norm, lq1, lk1, lq2, lk2,
                out_norm):
    tm, tq, tk, chunk = tiles
    s5w = S5_GROUPS * S5_GROUP
    lambda_init = 0.8 - 0.6 * math.exp(-0.3 * layer)
    qgain = _lane_col(jnp.tile(q_norm.astype(F32) * (DA_HEAD_DIM ** -0.5 * LOG2E), DA_HEADS))
    kgain = _lane_col(jnp.tile(k_norm, DA_HEADS))
    layout = (("f32", 0, 1), ("f32", 1, 1), ("qT", 2, 1), ("k", 3, 1), ("vT", 4, 1), ("f32", 5, 1))
    u, zs, qt, k, vt, zd = _project(
        x2d, e_norm, e_w_in.astype(BF16), qgain, kgain, batch=batch, seq=seq, tm=tm,
        layout=layout, head_dim=DA_HEAD_DIM)

    bmat, cmat, a_re, a_im = _s5_prepare(lam_re, lam_im, log_dt, b_re, b_im, c_re, c_im)
    y_s5 = _s5_mixer(u.reshape(batch, seq, s5w), zs.reshape(batch, seq, s5w), bmat, cmat,
                     a_re, a_im, d, w_glu.astype(BF16), b_glu, chunk=chunk)

    lam = (jnp.exp(jnp.sum(lq1.astype(F32) * lk1.astype(F32)))
           - jnp.exp(jnp.sum(lq2.astype(F32) * lk2.astype(F32))) + lambda_init)
    head_params = jnp.zeros((DA_HEADS, 8, LANES), F32)
    head_params = head_params.at[:, 0, :].set(_alibi_slopes(DA_HEADS)[:, None])
    head_params = head_params.at[:, 1, :].set(lam)
    y_da = _diff_attention(qt, k.reshape(batch, seq, -1), vt, head_params,
                           out_norm.astype(F32).reshape(1, HEAD_W),
                           zd.reshape(batch, seq, -1), tq=tq, tk=tk,
                           out_scale=1.0 - lambda_init)

    w_out = e_w_out.astype(BF16)
    return _out_project([y_s5.reshape(batch * seq, s5w), y_da.reshape(batch * seq, -1)],
                        [w_out[:s5w], w_out[s5w:]], x2d, tm=tm)


def _odd_layer(x2d, batch, seq, tiles, o_norm, o_w_in, o_b_f, o_w_out, q_norm, k_norm):
    tm, tq, tk, _ = tiles
    width = FOX_HEADS * HEAD_W
    nsec = width // SEC
    qgain = _lane_col(jnp.tile(q_norm.astype(F32) * (HEAD_W ** -0.5 * LOG2E), SEC // HEAD_W))
    kgain = _lane_col(jnp.tile(k_norm, SEC // HEAD_W))
    w = o_w_in.astype(BF16)
    wf = jnp.zeros((w.shape[0], LANES), BF16).at[:, :FOX_HEADS].set(w[:, 4 * width:])
    bf = jnp.zeros((1, LANES), F32).at[0, :FOX_HEADS].set(o_b_f.astype(F32))
    layout = (("qT", 0, nsec), ("k", nsec, nsec), ("vT", 2 * nsec, nsec), ("f32", 3 * nsec, nsec))
    qt, k, vt, z, logf = _project(
        x2d, o_norm, w[:, :4 * width], qgain, kgain, batch=batch, seq=seq, tm=tm,
        layout=layout, head_dim=HEAD_W, wf=wf, bf=bf)
    ct = _cumsum_t(logf.reshape(batch, seq, LANES), rows=tm)
    c2 = ct[:, :FOX_HEADS].reshape(batch, FOX_HEADS, seq // LANES, LANES)
    y = _fox_attention(qt, k.reshape(batch, seq, width), vt, c2, z.reshape(batch, seq, width),
                       tq=tq, tk=tk)
    return _out_project([y.reshape(batch * seq, width)], [o_w_out.astype(BF16)], x2d, tm=tm)


def _tiles(seq):
    tm = min(512, seq)
    return tm, min(2048, seq), tm, min(128, seq)


def kernel(x, e_norm, e_w_in, e_w_out, s5_lambda_re, s5_lambda_im, s5_log_dt, s5_b_re, s5_b_im, s5_c_re, s5_c_im, s5_d, s5_w_glu, s5_b_glu, da_q_norm, da_k_norm, da_lambda_q1, da_lambda_k1, da_lambda_q2, da_lambda_k2, da_out_norm, o_norm, o_w_in, o_b_f, o_w_out, fox_q_norm, fox_k_norm):
    batch, seq, d_model = x.shape
    tiles = _tiles(seq)
    depth = e_norm.shape[0] + o_norm.shape[0]
    x2d = x.reshape(batch * seq, d_model)
    for layer in range(depth):
        j = layer // 2
        if layer % 2 == 0:
            x2d = _even_layer(
                x2d, layer, batch, seq, tiles, e_norm[j], e_w_in[j], e_w_out[j],
                s5_lambda_re[j], s5_lambda_im[j], s5_log_dt[j], s5_b_re[j], s5_b_im[j],
                s5_c_re[j], s5_c_im[j], s5_d[j], s5_w_glu[j], s5_b_glu[j],
                da_q_norm[j], da_k_norm[j], da_lambda_q1[j], da_lambda_k1[j],
                da_lambda_q2[j], da_lambda_k2[j], da_out_norm[j])
        else:
            x2d = _odd_layer(x2d, batch, seq, tiles, o_norm[j], o_w_in[j], o_b_f[j], o_w_out[j],
                             fox_q_norm[j], fox_k_norm[j])
    return x2d.reshape(batch, seq, d_model)
```

```python
import functools
import math

import jax
import jax.numpy as jnp
from jax import lax
from jax.experimental import pallas as pl
from jax.experimental.pallas import tpu as pltpu

F32 = jnp.float32
BF16 = jnp.bfloat16
EPS = 1e-6
NEG = -1e30
LOG2E = math.log2(math.e)

LANES = 128
BF16_ROWS = 16
AUG = 256
CHUNK = 256
SCORES_AHEAD = 3
SEC = 1024
VMEM_LIMIT = 52 * 1024 * 1024

S5_GROUPS = 64
S5_GROUP = 16
S5_STATE = 64
S5_SLICES = 8
DA_HEADS = 8
DA_HEAD_DIM = 64
FOX_HEADS = 16
HEAD_W = 128
ACC_ROWS = HEAD_W + BF16_ROWS


def _params(sem):
    return pltpu.CompilerParams(dimension_semantics=sem, vmem_limit_bytes=VMEM_LIMIT)


def _head_norm_t(acc_t, gain_ref, head_dim):
    tm = acc_t.shape[1]
    r = acc_t.reshape(SEC // head_dim, head_dim, tm)
    ms = jnp.mean(r * r, axis=1, keepdims=True)
    n = (r * lax.rsqrt(ms + EPS)).reshape(SEC, tm)
    g = gain_ref[...]
    return jnp.concatenate(
        [n[:, c * LANES:(c + 1) * LANES] * g for c in range(tm // LANES)], axis=1)


def _proj_kernel(*refs, layout, head_dim, has_logf):
    it = iter(refs)
    x_ref, g_ref, w_ref = next(it), next(it), next(it)
    if has_logf:
        wf_ref, bf_ref = next(it), next(it)
    qg_ref, kg_ref = next(it), next(it)
    out_refs = [next(it) for _ in layout]
    if has_logf:
        logf_ref = next(it)
    hs_ref = next(it)

    j = pl.program_id(1)

    @pl.when(j == 0)
    def _():
        x = x_ref[...]
        ms = jnp.mean(x * x, axis=-1, keepdims=True)
        hs_ref[...] = (x * lax.rsqrt(ms + EPS) * g_ref[...]).astype(BF16)
        if has_logf:
            f = jnp.dot(hs_ref[...], wf_ref[...], preferred_element_type=F32) + bf_ref[...]
            logf_ref[...] = jax.nn.log_sigmoid(f)

    acc = jnp.dot(hs_ref[...], w_ref[...], preferred_element_type=F32)

    for (kind, j0, cnt), o_ref in zip(layout, out_refs):
        @pl.when((j >= j0) & (j < j0 + cnt))
        def _(kind=kind, o_ref=o_ref):
            if kind == "f32":
                o_ref[...] = acc
            elif kind == "vT":
                o_ref[...] = acc.T.astype(BF16)
            elif kind == "qT":
                o_ref[...] = _head_norm_t(acc.T, qg_ref, head_dim).astype(BF16)
            else:
                o_ref[...] = _head_norm_t(acc.T, kg_ref, head_dim).T.astype(BF16)


def _project(x2d, norm_g, w, qgain, kgain, *, batch, seq, tm, layout, head_dim, wf=None, bf=None):
    n_rows, d = x2d.shape
    nsec = w.shape[1] // SEC
    nt = seq // tm
    has_logf = wf is not None

    def sec(j, j0, cnt):
        return jnp.clip(j - j0, 0, cnt - 1)

    in_specs = [
        pl.BlockSpec((tm, d), lambda i, j: (i, 0)),
        pl.BlockSpec((1, d), lambda i, j: (0, 0)),
        pl.BlockSpec((d, SEC), lambda i, j: (0, j)),
    ]
    args = [x2d, norm_g.reshape(1, d), w]
    if has_logf:
        in_specs += [pl.BlockSpec((d, LANES), lambda i, j: (0, 0)),
                     pl.BlockSpec((1, LANES), lambda i, j: (0, 0))]
        args += [wf, bf]
    in_specs += [pl.BlockSpec((SEC, LANES), lambda i, j: (0, 0)),
                 pl.BlockSpec((SEC, LANES), lambda i, j: (0, 0))]
    args += [qgain, kgain]

    out_shapes, out_specs = [], []
    for kind, j0, cnt in layout:
        if kind == "f32" or kind == "k":
            dt = F32 if kind == "f32" else BF16
            out_shapes.append(jax.ShapeDtypeStruct((n_rows, cnt * SEC), dt))
            out_specs.append(pl.BlockSpec(
                (tm, SEC), lambda i, j, j0=j0, cnt=cnt: (i, sec(j, j0, cnt))))
        elif kind == "qT":
            out_shapes.append(jax.ShapeDtypeStruct((batch, cnt * SEC, seq), BF16))
            out_specs.append(pl.BlockSpec(
                (None, SEC, tm), lambda i, j, j0=j0, cnt=cnt: (i // nt, sec(j, j0, cnt), i % nt)))
        else:
            out_shapes.append(jax.ShapeDtypeStruct((batch, nt, cnt * SEC, tm), BF16))
            out_specs.append(pl.BlockSpec(
                (None, None, SEC, tm),
                lambda i, j, j0=j0, cnt=cnt: (i // nt, i % nt, sec(j, j0, cnt), 0)))
    if has_logf:
        out_shapes.append(jax.ShapeDtypeStruct((n_rows, LANES), F32))
        out_specs.append(pl.BlockSpec((tm, LANES), lambda i, j: (i, 0)))

    return pl.pallas_call(
        functools.partial(_proj_kernel, layout=layout, head_dim=head_dim, has_logf=has_logf),
        grid=(n_rows // tm, nsec),
        in_specs=in_specs,
        out_specs=out_specs,
        out_shape=out_shapes,
        scratch_shapes=[pltpu.VMEM((tm, d), BF16)],
        compiler_params=_params(("arbitrary", "arbitrary")),
        name="rms_in_proj",
    )(*args)


def _out_kernel(*refs, n_in):
    ys, ws = refs[:n_in], refs[n_in:2 * n_in]
    x_ref, o_ref = refs[2 * n_in], refs[2 * n_in + 1]
    acc = x_ref[...]
    for y_ref, w_ref in zip(ys, ws):
        acc = acc + jnp.dot(y_ref[...], w_ref[...], preferred_element_type=F32)
    o_ref[...] = acc


def _out_project(ys, ws, x2d, *, tm):
    n_rows, d = x2d.shape
    n_in = len(ys)
    in_specs = [pl.BlockSpec((tm, y.shape[1]), lambda i: (i, 0)) for y in ys]
    in_specs += [pl.BlockSpec(w.shape, lambda i: (0, 0)) for w in ws]
    in_specs += [pl.BlockSpec((tm, d), lambda i: (i, 0))]
    return pl.pallas_call(
        functools.partial(_out_kernel, n_in=n_in),
        grid=(n_rows // tm,),
        in_specs=in_specs,
        out_specs=pl.BlockSpec((tm, d), lambda i: (i, 0)),
        out_shape=jax.ShapeDtypeStruct((n_rows, d), F32),
        compiler_params=_params(("arbitrary",)),
        name="out_proj_residual",
    )(*ys, *ws, x2d)


def _s5_prepare(lam_re, lam_im, log_dt, b_re, b_im, c_re, c_im):
    dt = jnp.exp(log_dt.astype(F32))[:, None]
    lr, li = lam_re.astype(F32), lam_im.astype(F32)
    mag = jnp.exp(lr * dt)
    ar, ai = mag * jnp.cos(li * dt), mag * jnp.sin(li * dt)
    den = lr * lr + li * li
    nr, ni = ar - 1.0, ai
    kr = (nr * lr + ni * li) / den
    ki = (ni * lr - nr * li) / den
    br, bi = b_re.astype(F32), b_im.astype(F32)
    bbr = kr[..., None] * br - ki[..., None] * bi
    bbi = kr[..., None] * bi + ki[..., None] * br
    nblk = S5_GROUPS // 16
    eye = jnp.eye(16, dtype=F32)

    def b_tiles(bb):
        t = bb.reshape(nblk, 16, S5_SLICES, 8, S5_GROUP)
        return jnp.einsum("ngkph,fg->knfhgp", t, eye).reshape(S5_SLICES, nblk, 256, LANES)

    def c_tiles(cc):
        t = cc.reshape(nblk, 16, S5_GROUP, S5_SLICES, 8)
        return jnp.einsum("nghkp,fg->nkgpfh", t, eye).reshape(nblk, S5_SLICES, LANES, 256)

    bmat = jnp.concatenate([b_tiles(bbr), b_tiles(bbi)], axis=-1).astype(BF16)
    cmat = jnp.concatenate([c_tiles(c_re.astype(F32)), c_tiles(-c_im.astype(F32))], axis=2)
    cmat = cmat.reshape(nblk, S5_SLICES * 256, 256).astype(BF16)

    def a_tiles(a):
        t = a.reshape(nblk, 16, S5_SLICES, 8).transpose(2, 0, 1, 3)
        return t.reshape(S5_SLICES, nblk * LANES)

    return bmat, cmat, a_tiles(ar), a_tiles(ai)


def _s5_kernel(u_ref, z_ref, bm_ref, cm_ref, ar_ref, ai_ref, d_ref, wg_ref, bg_ref,
               o_ref, s_ref, h_ref, *, chunk, pitch):
    nb = u_ref.shape[0]
    nblk = ar_ref.shape[1] // LANES

    @pl.when(pl.program_id(0) == 0)
    def _():
        h_ref[...] = jnp.zeros_like(h_ref)

    for b in range(nb):
        ub = u_ref[b].astype(BF16)
        for k in range(S5_SLICES):
            for n in range(nblk):
                x = jnp.dot(ub[:, n * 256:(n + 1) * 256], bm_ref[k, n],
                            preferred_element_type=F32)
                s_ref[b, 2 * n, k * pitch:k * pitch + chunk, :] = x[:, :LANES]
                s_ref[b, 2 * n + 1, k * pitch:k * pitch + chunk, :] = x[:, LANES:]

    ar = [ar_ref[:, n * LANES:(n + 1) * LANES] for n in range(nblk)]
    ai = [ai_ref[:, n * LANES:(n + 1) * LANES] for n in range(nblk)]

    def step(t, hs):
        out = []
        for b in range(nb):
            for n in range(nblk):
                hr, hi = hs[2 * (b * nblk + n)], hs[2 * (b * nblk + n) + 1]
                rows = pl.ds(t, S5_SLICES, stride=pitch)
                xr = s_ref[b, 2 * n, rows, :]
                xi = s_ref[b, 2 * n + 1, rows, :]
                nr = ar[n] * hr - ai[n] * hi + xr
                ni = ar[n] * hi + ai[n] * hr + xi
                s_ref[b, 2 * n, rows, :] = nr
                s_ref[b, 2 * n + 1, rows, :] = ni
                out += [nr, ni]
        return tuple(out)

    h0 = tuple(h_ref[b, s] for b in range(nb) for s in range(2 * nblk))
    hs = lax.fori_loop(0, chunk, step, h0, unroll=4)
    for b in range(nb):
        for s in range(2 * nblk):
            h_ref[b, s] = hs[b * 2 * nblk + s]

    for b in range(nb):
        ys = []
        for n in range(nblk):
            acc = None
            for k in range(S5_SLICES):
                rows = slice(k * pitch, k * pitch + chunk)
                hk = jnp.concatenate([s_ref[b, 2 * n, rows, :], s_ref[b, 2 * n + 1, rows, :]],
                                     axis=1).astype(BF16)
                part = jnp.dot(hk, cm_ref[n, k * 256:(k + 1) * 256, :],
                               preferred_element_type=F32)
                acc = part if acc is None else acc + part
            ys.append(acc)
        y = jnp.concatenate(ys, axis=1) + d_ref[...] * u_ref[b]
        y = jax.nn.gelu(y)
        gate = jax.nn.sigmoid(
            jnp.dot(y.astype(BF16), wg_ref[...], preferred_element_type=F32) + bg_ref[...])
        o_ref[b] = (y * gate * jax.nn.silu(z_ref[b])).astype(BF16)


def _s5_mixer(u, z, bmat, cmat, a_re, a_im, d, w_glu, b_glu, *, chunk):
    batch, seq, width = u.shape
    pitch = chunk + 8
    nblk = width // 256
    const = lambda shape: pl.BlockSpec(shape, lambda c: (0,) * len(shape))
    return pl.pallas_call(
        functools.partial(_s5_kernel, chunk=chunk, pitch=pitch),
        grid=(seq // chunk,),
        in_specs=[
            pl.BlockSpec((batch, chunk, width), lambda c: (0, c, 0)),
            pl.BlockSpec((batch, chunk, width), lambda c: (0, c, 0)),
            const(bmat.shape), const(cmat.shape), const(a_re.shape), const(a_im.shape),
            const((1, width)), const(w_glu.shape), const((1, width)),
        ],
        out_specs=pl.BlockSpec((batch, chunk, width), lambda c: (0, c, 0)),
        out_shape=jax.ShapeDtypeStruct((batch, seq, width), BF16),
        scratch_shapes=[
            pltpu.VMEM((batch, 2 * nblk, S5_SLICES * pitch, LANES), F32),
            pltpu.VMEM((batch, 2 * nblk, S5_SLICES, LANES), F32),
        ],
        compiler_params=_params(("arbitrary",)),
        name="s5_mixer",
    )(u, z, bmat, cmat, a_re, a_im, d.reshape(1, width), w_glu, b_glu.reshape(1, width))


def _split3(x):
    hi = x.astype(BF16).astype(F32)
    r = x - hi
    mid = r.astype(BF16).astype(F32)
    return hi, mid, r - mid


def _select_rows(index, values):
    out = jnp.zeros(index.shape, F32)
    for i, v in enumerate(values):
        out = jnp.where(index == i, v, out)
    return out


class _Chain:
    def __init__(self, kaug_blk, vt_blk, qaug_ref, stats, c, mask_off):
        self.kaug_blk, self.vt_blk, self.qaug_ref = kaug_blk, vt_blk, qaug_ref
        self.m_ref, self.acc_ref = stats
        self.lanes = slice(c * CHUNK, (c + 1) * CHUNK)
        self.mask_off = mask_off

    def scores(self):
        s = jnp.dot(self.kaug_blk, self.qaug_ref[:, self.lanes],
                    preferred_element_type=F32)
        if self.mask_off is not None:
            key = lax.broadcasted_iota(jnp.int32, s.shape, 0)
            qry = lax.broadcasted_iota(jnp.int32, s.shape, 1)
            s = jnp.where(key - qry <= self.mask_off, s, NEG)
        return s

    def softmax(self, t):
        m_prev = self.m_ref[:, self.lanes]
        m_new = jnp.maximum(m_prev, jnp.max(t, axis=0, keepdims=True))
        alpha = jnp.exp2(m_prev - m_new)
        p = jnp.exp2(t - m_new)
        self.m_ref[:, self.lanes] = m_new
        return p.astype(BF16), alpha

    def values(self, p, alpha):
        pv = jnp.dot(self.vt_blk, p, preferred_element_type=F32)
        self.acc_ref[:, self.lanes] = alpha * self.acc_ref[:, self.lanes] + pv


def _run_chains(chains):
    ready = [ch.scores() for ch in chains[:SCORES_AHEAD]]
    for i, ch in enumerate(chains):
        if i + SCORES_AHEAD < len(chains):
            ready.append(chains[i + SCORES_AHEAD].scores())
        p, alpha = ch.softmax(ready.pop(0))
        ch.values(p, alpha)


def _causal_sweep(qi, tq, tk, load, make):
    per, nch = tq // tk, tq // CHUNK

    def body(j, carry):
        chains = []
        for d in range(per):
            blk = load(j * per + d)
            for c in range(nch):
                chains += make(blk, c, None)
        _run_chains(chains)
        return carry

    lax.fori_loop(0, qi, body, 0)
    chains = []
    for d in range(per):
        blk = load(qi * per + d)
        for c in range(nch):
            off = c * CHUNK - d * tk
            if off < -(CHUNK - 1):
                continue
            chains += make(blk, c, None if off >= tk - 1 else off)
    _run_chains(chains)


def _init_stats(*pairs):
    for m_ref, a_ref in pairs:
        m_ref[...] = jnp.full_like(m_ref, NEG)
        a_ref[...] = jnp.zeros_like(a_ref)


def _values_with_ones(vt_blk):
    row = lax.broadcasted_iota(jnp.int32, (BF16_ROWS, vt_blk.shape[1]), 0)
    ones = jnp.where(row == 0, 1.0, 0.0).astype(BF16)
    return jnp.concatenate([vt_blk, ones], axis=0)


def _normalised(acc_ref):
    return acc_ref[:HEAD_W, :] * (1.0 / acc_ref[HEAD_W:HEAD_W + 1, :])


def _fox_kernel(qt_ref, k_ref, vt_ref, c_ref, z_ref, o_ref,
                kaug_ref, qaug_ref, m_ref, acc_ref, *, tq, tk):
    qi = pl.program_id(2)

    @pl.when(qi == 0)
    def _():
        lane = lax.broadcasted_iota(jnp.int32, (LANES, LANES), 1)

        def fill(r, carry):
            rows = pl.ds(pl.multiple_of(r * LANES, LANES), LANES)
            col = jnp.broadcast_to(c_ref[pl.ds(r, 1), :], (LANES, LANES)).T
            hi, mid, lo = _split3(col * (-LOG2E))
            ext = _select_rows(lane, (hi, mid, lo, 1.0, 1.0, 1.0))
            kaug_ref[rows, :LANES] = k_ref[rows, :]
            kaug_ref[rows, LANES:] = ext.astype(BF16)
            return carry

        lax.fori_loop(0, c_ref.shape[0], fill, 0, unroll=8)

    crow = c_ref[pl.ds(qi * (tq // LANES), 1), :]
    c0 = jnp.broadcast_to(crow[:, 0:1], (BF16_ROWS, tq)) * LOG2E
    hi, mid, lo = _split3(c0)
    row = lax.broadcasted_iota(jnp.int32, (BF16_ROWS, tq), 0)
    qaug_ref[:HEAD_W, :] = qt_ref[...]
    qaug_ref[HEAD_W:HEAD_W + BF16_ROWS, :] = _select_rows(row, (1.0, 1.0, 1.0, hi, mid, lo)).astype(BF16)
    qaug_ref[HEAD_W + BF16_ROWS:, :] = jnp.zeros((AUG - HEAD_W - BF16_ROWS, tq), BF16)
    _init_stats((m_ref, acc_ref))

    def load(kb):
        ks = pl.multiple_of(kb * tk, tk)
        return kaug_ref[pl.ds(ks, tk), :], _values_with_ones(vt_ref[kb])

    def make(blk, c, mask_off):
        return [_Chain(blk[0], blk[1], qaug_ref, (m_ref, acc_ref), c, mask_off)]

    _causal_sweep(qi, tq, tk, load, make)

    o = _normalised(acc_ref).T
    o_ref[...] = (o * jax.nn.silu(z_ref[...])).astype(BF16)


def _fox_attention(qt, k, vt, c2, z, *, tq, tk):
    batch, width, seq = qt.shape
    heads = width // HEAD_W
    return pl.pallas_call(
        functools.partial(_fox_kernel, tq=tq, tk=tk),
        grid=(batch, heads, seq // tq),
        in_specs=[
            pl.BlockSpec((None, HEAD_W, tq), lambda b, h, q: (b, h, q)),
            pl.BlockSpec((None, seq, HEAD_W), lambda b, h, q: (b, 0, h)),
            pl.BlockSpec((None, seq // tk, HEAD_W, tk), lambda b, h, q: (b, 0, h, 0)),
            pl.BlockSpec((None, None, seq // LANES, LANES), lambda b, h, q: (b, h, 0, 0)),
            pl.BlockSpec((None, tq, HEAD_W), lambda b, h, q: (b, q, h)),
        ],
        out_specs=pl.BlockSpec((None, tq, HEAD_W), lambda b, h, q: (b, q, h)),
        out_shape=jax.ShapeDtypeStruct((batch, seq, width), BF16),
        scratch_shapes=[
            pltpu.VMEM((seq, AUG), BF16), pltpu.VMEM((AUG, tq), BF16),
            pltpu.VMEM((1, tq), F32), pltpu.VMEM((ACC_ROWS, tq), F32),
        ],
        compiler_params=_params(("arbitrary", "arbitrary", "arbitrary")),
        name="fox_attention",
    )(qt, k, vt, c2, z)


def _diff_kernel(qt_ref, k_ref, vt_ref, hp_ref, og_ref, z_ref, o_ref,
                 kaug_ref, qa0_ref, qa1_ref, m0_ref, a0_ref, m1_ref, a1_ref,
                 *, tq, tk, out_scale):
    qi = pl.program_id(2)
    dh = qt_ref.shape[0] // 2
    slope2 = hp_ref[0:1, :] * LOG2E
    lam = hp_ref[1:2, 0:1]

    @pl.when(qi == 0)
    def _():
        lane = lax.broadcasted_iota(jnp.int32, (LANES, LANES), 1)
        key = lax.broadcasted_iota(jnp.int32, (LANES, LANES), 0)

        def fill(r, carry):
            rows = pl.ds(pl.multiple_of(r * LANES, LANES), LANES)
            base = (key * 0 + r * LANES).astype(F32)
            ext = _select_rows(lane, _split3(slope2 * base) + _split3(slope2 * key.astype(F32))
                               + (1.0, 1.0, 1.0))
            kaug_ref[rows, :LANES] = k_ref[rows, :]
            kaug_ref[rows, LANES:] = ext.astype(BF16)
            return carry

        lax.fori_loop(0, k_ref.shape[0] // LANES, fill, 0)

    row = lax.broadcasted_iota(jnp.int32, (BF16_ROWS, tq), 0)
    q0 = (row * 0 + qi * tq).astype(F32)
    slope_t = jnp.concatenate([slope2] * (tq // LANES), axis=1)
    ext = _select_rows(row, (1.0,) * 6 + _split3(-slope_t * q0)).astype(BF16)
    qt = qt_ref[...]
    first = lax.broadcasted_iota(jnp.int32, qt.shape, 0) < dh
    zero = jnp.zeros_like(qt)
    for qa_ref, comp in ((qa0_ref, jnp.where(first, qt, zero)), (qa1_ref, jnp.where(first, zero, qt))):
        qa_ref[:HEAD_W, :] = comp
        qa_ref[HEAD_W:HEAD_W + BF16_ROWS, :] = ext
        qa_ref[HEAD_W + BF16_ROWS:, :] = jnp.zeros((AUG - HEAD_W - BF16_ROWS, tq), BF16)
    _init_stats((m0_ref, a0_ref), (m1_ref, a1_ref))

    def load(kb):
        ks = pl.multiple_of(kb * tk, tk)
        return kaug_ref[pl.ds(ks, tk), :], _values_with_ones(vt_ref[kb])

    def make(blk, c, mask_off):
        return [_Chain(blk[0], blk[1], qa0_ref, (m0_ref, a0_ref), c, mask_off),
                _Chain(blk[0], blk[1], qa1_ref, (m1_ref, a1_ref), c, mask_off)]

    _causal_sweep(qi, tq, tk, load, make)

    ot = _normalised(a0_ref) - lam * _normalised(a1_ref)
    ms = jnp.mean(ot * ot, axis=0, keepdims=True)
    o = (ot * lax.rsqrt(ms + EPS)).T
    o = o * og_ref[...] * out_scale
    o_ref[...] = (o * jax.nn.silu(z_ref[...])).astype(BF16)


def _diff_attention(qt, k, vt, head_params, out_gain, z, *, tq, tk, out_scale):
    batch, width, seq = qt.shape
    heads = width // HEAD_W
    stat = lambda: pltpu.VMEM((1, tq), F32)
    accum = lambda: pltpu.VMEM((ACC_ROWS, tq), F32)
    return pl.pallas_call(
        functools.partial(_diff_kernel, tq=tq, tk=tk, out_scale=out_scale),
        grid=(batch, heads, seq // tq),
        in_specs=[
            pl.BlockSpec((None, HEAD_W, tq), lambda b, h, q: (b, h, q)),
            pl.BlockSpec((None, seq, HEAD_W), lambda b, h, q: (b, 0, h)),
            pl.BlockSpec((None, seq // tk, HEAD_W, tk), lambda b, h, q: (b, 0, h, 0)),
            pl.BlockSpec((None, 8, LANES), lambda b, h, q: (h, 0, 0)),
            pl.BlockSpec((1, HEAD_W), lambda b, h, q: (0, 0)),
            pl.BlockSpec((None, tq, HEAD_W), lambda b, h, q: (b, q, h)),
        ],
        out_specs=pl.BlockSpec((None, tq, HEAD_W), lambda b, h, q: (b, q, h)),
        out_shape=jax.ShapeDtypeStruct((batch, seq, width), BF16),
        scratch_shapes=[pltpu.VMEM((seq, AUG), BF16),
                        pltpu.VMEM((AUG, tq), BF16), pltpu.VMEM((AUG, tq), BF16),
                        stat(), accum(), stat(), accum()],
        compiler_params=_params(("arbitrary", "arbitrary", "arbitrary")),
        name="diff_attention",
    )(qt, k, vt, head_params, out_gain, z)


def _cumsum_kernel(lf_ref, o_ref, carry_ref, *, rows):
    @pl.when(pl.program_id(1) == 0)
    def _():
        carry_ref[...] = jnp.zeros_like(carry_ref)

    x = lf_ref[...]
    r = lax.broadcasted_iota(jnp.int32, (rows, rows), 0)
    c = lax.broadcasted_iota(jnp.int32, (rows, rows), 1)
    tri = (c <= r).astype(BF16)
    hi = x.astype(BF16)
    r1 = x - hi.astype(F32)
    mid = r1.astype(BF16)
    lo = (r1 - mid.astype(F32)).astype(BF16)
    cum = (jnp.dot(tri, hi, preferred_element_type=F32)
           + jnp.dot(tri, mid, preferred_element_type=F32)
           + jnp.dot(tri, lo, preferred_element_type=F32)) + carry_ref[...]
    carry_ref[...] = cum[rows - 1:rows, :]
    o_ref[...] = cum.T


def _cumsum_t(logf, *, rows):
    batch, seq, _ = logf.shape
    return pl.pallas_call(
        functools.partial(_cumsum_kernel, rows=rows),
        grid=(batch, seq // rows),
        in_specs=[pl.BlockSpec((None, rows, LANES), lambda b, i: (b, i, 0))],
        out_specs=pl.BlockSpec((None, LANES, rows), lambda b, i: (b, 0, i)),
        out_shape=jax.ShapeDtypeStruct((batch, LANES, seq), F32),
        scratch_shapes=[pltpu.VMEM((1, LANES), F32)],
        compiler_params=_params(("arbitrary", "arbitrary")),
        name="logf_cumsum",
    )(logf)


def _lane_col(v):
    return jnp.broadcast_to(v.astype(F32)[:, None], (v.shape[0], LANES))


def _alibi_slopes(n):
    return jnp.asarray([2.0 ** (-8.0 * (h + 1) / n) for h in range(n)], F32)


def _even_layer(x2d, layer, batch, seq, tiles, e_norm, e_w_in, e_w_out, lam_re, lam_im, log_dt,
                b_re, b_im, c_re, c_im, d, w_glu, b_glu, q_norm, k_norm, lq1, lk1, lq2, lk2,
                out_norm):
    tm, tq, tk, chunk = tiles
    s5w = S5_GROUPS * S5_GROUP
    lambda_init = 0.8 - 0.6 * math.exp(-0.3 * layer)
    qgain = _lane_col(jnp.tile(q_norm.astype(F32) * (DA_HEAD_DIM ** -0.5 * LOG2E), DA_HEADS))
    kgain = _lane_col(jnp.tile(k_norm, DA_HEADS))
    layout = (("f32", 0, 1), ("f32", 1, 1), ("qT", 2, 1), ("k", 3, 1), ("vT", 4, 1), ("f32", 5, 1))
    u, zs, qt, k, vt, zd = _project(
        x2d, e_norm, e_w_in.astype(BF16), qgain, kgain, batch=batch, seq=seq, tm=tm,
        layout=layout, head_dim=DA_HEAD_DIM)

    bmat, cmat, a_re, a_im = _s5_prepare(lam_re, lam_im, log_dt, b_re, b_im, c_re, c_im)
    y_s5 = _s5_mixer(u.reshape(batch, seq, s5w), zs.reshape(batch, seq, s5w), bmat, cmat,
                     a_re, a_im, d, w_glu.astype(BF16), b_glu, chunk=chunk)

    lam = (jnp.exp(jnp.sum(lq1.astype(F32) * lk1.astype(F32)))
           - jnp.exp(jnp.sum(lq2.astype(F32) * lk2.astype(F32))) + lambda_init)
    head_params = jnp.zeros((DA_HEADS, 8, LANES), F32)
    head_params = head_params.at[:, 0, :].set(_alibi_slopes(DA_HEADS)[:, None])
    head_params = head_params.at[:, 1, :].set(lam)
    y_da = _diff_attention(qt, k.reshape(batch, seq, -1), vt, head_params,
                           out_norm.astype(F32).reshape(1, HEAD_W),
                           zd.reshape(batch, seq, -1), tq=tq, tk=tk,
                           out_scale=1.0 - lambda_init)

    w_out = e_w_out.astype(BF16)
    return _out_project([y_s5.reshape(batch * seq, s5w), y_da.reshape(batch * seq, -1)],
                        [w_out[:s5w], w_out[s5w:]], x2d, tm=tm)


def _odd_layer(x2d, batch, seq, tiles, o_norm, o_w_in, o_b_f, o_w_out, q_norm, k_norm):
    tm, tq, tk, _ = tiles
    width = FOX_HEADS * HEAD_W
    nsec = width // SEC
    qgain = _lane_col(jnp.tile(q_norm.astype(F32) * (HEAD_W ** -0.5 * LOG2E), SEC // HEAD_W))
    kgain = _lane_col(jnp.tile(k_norm, SEC // HEAD_W))
    w = o_w_in.astype(BF16)
    wf = jnp.zeros((w.shape[0], LANES), BF16).at[:, :FOX_HEADS].set(w[:, 4 * width:])
    bf = jnp.zeros((1, LANES), F32).at[0, :FOX_HEADS].set(o_b_f.astype(F32))
    layout = (("qT", 0, nsec), ("k", nsec, nsec), ("vT", 2 * nsec, nsec), ("f32", 3 * nsec, nsec))
    qt, k, vt, z, logf = _project(
        x2d, o_norm, w[:, :4 * width], qgain, kgain, batch=batch, seq=seq, tm=tm,
        layout=layout, head_dim=HEAD_W, wf=wf, bf=bf)
    ct = _cumsum_t(logf.reshape(batch, seq, LANES), rows=tm)
    c2 = ct[:, :FOX_HEADS].reshape(batch, FOX_HEADS, seq // LANES, LANES)
    y = _fox_attention(qt, k.reshape(batch, seq, width), vt, c2, z.reshape(batch, seq, width),
                       tq=min(2 * tq, seq), tk=tk)
    return _out_project([y.reshape(batch * seq, width)], [o_w_out.astype(BF16)], x2d, tm=tm)


def _tiles(seq):
    tm = min(512, seq)
    return tm, min(2048, seq), tm, min(128, seq)


def kernel(x, e_norm, e_w_in, e_w_out, s5_lambda_re, s5_lambda_im, s5_log_dt, s5_b_re, s5_b_im, s5_c_re, s5_c_im, s5_d, s5_w_glu, s5_b_glu, da_q_norm, da_k_norm, da_lambda_q1, da_lambda_k1, da_lambda_q2, da_lambda_k2, da_out_norm, o_norm, o_w_in, o_b_f, o_w_out, fox_q_norm, fox_k_norm):
    batch, seq, d_model = x.shape
    tiles = _tiles(seq)
    depth = e_norm.shape[0] + o_norm.shape[0]
    x2d = x.reshape(batch * seq, d_model)
    for layer in range(depth):
        j = layer // 2
        if layer % 2 == 0:
            x2d = _even_layer(
                x2d, layer, batch, seq, tiles, e_norm[j], e_w_in[j], e_w_out[j],
                s5_lambda_re[j], s5_lambda_im[j], s5_log_dt[j], s5_b_re[j], s5_b_im[j],
                s5_c_re[j], s5_c_im[j], s5_d[j], s5_w_glu[j], s5_b_glu[j],
                da_q_norm[j], da_k_norm[j], da_lambda_q1[j], da_lambda_k1[j],
                da_lambda_q2[j], da_lambda_k2[j], da_out_norm[j])
        else:
            x2d = _odd_layer(x2d, batch, seq, tiles, o_norm[j], o_w_in[j], o_b_f[j], o_w_out[j],
                             fox_q_norm[j], fox_k_norm[j])
    return x2d.reshape(batch, seq, d_model)
```

```python
import functools
import math

import jax
import jax.numpy as jnp
from jax import lax
from jax.experimental import pallas as pl
from jax.experimental.pallas import tpu as pltpu

F32 = jnp.float32
BF16 = jnp.bfloat16
EPS = 1e-6
NEG = -1e30
LOG2E = math.log2(math.e)

LANES = 128
BF16_ROWS = 16
AUG = 256
CHUNK = 256
SCORES_AHEAD = 3
SEC = 1024
VMEM_LIMIT = 52 * 1024 * 1024

S5_GROUPS = 64
S5_GROUP = 16
S5_STATE = 64
S5_SLICES = 8
DA_HEADS = 8
DA_HEAD_DIM = 64
FOX_HEADS = 16
HEAD_W = 128
ACC_ROWS = HEAD_W + BF16_ROWS


def _params(sem):
    return pltpu.CompilerParams(dimension_semantics=sem, vmem_limit_bytes=VMEM_LIMIT)


def _head_norm_t(acc_t, gain_ref, head_dim):
    tm = acc_t.shape[1]
    r = acc_t.reshape(SEC // head_dim, head_dim, tm)
    ms = jnp.mean(r * r, axis=1, keepdims=True)
    n = (r * lax.rsqrt(ms + EPS)).reshape(SEC, tm)
    g = gain_ref[...]
    return jnp.concatenate(
        [n[:, c * LANES:(c + 1) * LANES] * g for c in range(tm // LANES)], axis=1)


def _proj_kernel(*refs, layout, head_dim, has_logf):
    it = iter(refs)
    x_ref, g_ref, w_ref = next(it), next(it), next(it)
    if has_logf:
        wf_ref, bf_ref = next(it), next(it)
    qg_ref, kg_ref = next(it), next(it)
    out_refs = [next(it) for _ in layout]
    if has_logf:
        logf_ref = next(it)
    hs_ref = next(it)

    j = pl.program_id(1)

    @pl.when(j == 0)
    def _():
        x = x_ref[...]
        ms = jnp.mean(x * x, axis=-1, keepdims=True)
        hs_ref[...] = (x * lax.rsqrt(ms + EPS) * g_ref[...]).astype(BF16)
        if has_logf:
            f = jnp.dot(hs_ref[...], wf_ref[...], preferred_element_type=F32) + bf_ref[...]
            logf_ref[...] = jax.nn.log_sigmoid(f)

    acc = jnp.dot(hs_ref[...], w_ref[...], preferred_element_type=F32)

    for (kind, j0, cnt), o_ref in zip(layout, out_refs):
        @pl.when((j >= j0) & (j < j0 + cnt))
        def _(kind=kind, o_ref=o_ref):
            if kind == "f32":
                o_ref[...] = acc
            elif kind == "vT":
                o_ref[...] = acc.T.astype(BF16)
            elif kind == "qT":
                o_ref[...] = _head_norm_t(acc.T, qg_ref, head_dim).astype(BF16)
            else:
                o_ref[...] = _head_norm_t(acc.T, kg_ref, head_dim).T.astype(BF16)


def _project(x2d, norm_g, w, qgain, kgain, *, batch, seq, tm, layout, head_dim, wf=None, bf=None):
    n_rows, d = x2d.shape
    nsec = w.shape[1] // SEC
    nt = seq // tm
    has_logf = wf is not None

    def sec(j, j0, cnt):
        return jnp.clip(j - j0, 0, cnt - 1)

    in_specs = [
        pl.BlockSpec((tm, d), lambda i, j: (i, 0)),
        pl.BlockSpec((1, d), lambda i, j: (0, 0)),
        pl.BlockSpec((d, SEC), lambda i, j: (0, j)),
    ]
    args = [x2d, norm_g.reshape(1, d), w]
    if has_logf:
        in_specs += [pl.BlockSpec((d, LANES), lambda i, j: (0, 0)),
                     pl.BlockSpec((1, LANES), lambda i, j: (0, 0))]
        args += [wf, bf]
    in_specs += [pl.BlockSpec((SEC, LANES), lambda i, j: (0, 0)),
                 pl.BlockSpec((SEC, LANES), lambda i, j: (0, 0))]
    args += [qgain, kgain]

    out_shapes, out_specs = [], []
    for kind, j0, cnt in layout:
        if kind == "f32" or kind == "k":
            dt = F32 if kind == "f32" else BF16
            out_shapes.append(jax.ShapeDtypeStruct((n_rows, cnt * SEC), dt))
            out_specs.append(pl.BlockSpec(
                (tm, SEC), lambda i, j, j0=j0, cnt=cnt: (i, sec(j, j0, cnt))))
        elif kind == "qT":
            out_shapes.append(jax.ShapeDtypeStruct((batch, cnt * SEC, seq), BF16))
            out_specs.append(pl.BlockSpec(
                (None, SEC, tm), lambda i, j, j0=j0, cnt=cnt: (i // nt, sec(j, j0, cnt), i % nt)))
        else:
            out_shapes.append(jax.ShapeDtypeStruct((batch, nt, cnt * SEC, tm), BF16))
            out_specs.append(pl.BlockSpec(
                (None, None, SEC, tm),
                lambda i, j, j0=j0, cnt=cnt: (i // nt, i % nt, sec(j, j0, cnt), 0)))
    if has_logf:
        out_shapes.append(jax.ShapeDtypeStruct((n_rows, LANES), F32))
        out_specs.append(pl.BlockSpec((tm, LANES), lambda i, j: (i, 0)))

    return pl.pallas_call(
        functools.partial(_proj_kernel, layout=layout, head_dim=head_dim, has_logf=has_logf),
        grid=(n_rows // tm, nsec),
        in_specs=in_specs,
        out_specs=out_specs,
        out_shape=out_shapes,
        scratch_shapes=[pltpu.VMEM((tm, d), BF16)],
        compiler_params=_params(("arbitrary", "arbitrary")),
        name="rms_in_proj",
    )(*args)


def _out_kernel(*refs, n_in):
    ys, ws = refs[:n_in], refs[n_in:2 * n_in]
    x_ref, o_ref = refs[2 * n_in], refs[2 * n_in + 1]
    acc = x_ref[...]
    for y_ref, w_ref in zip(ys, ws):
        acc = acc + jnp.dot(y_ref[...], w_ref[...], preferred_element_type=F32)
    o_ref[...] = acc


def _out_project(ys, ws, x2d, *, tm):
    n_rows, d = x2d.shape
    n_in = len(ys)
    in_specs = [pl.BlockSpec((tm, y.shape[1]), lambda i: (i, 0)) for y in ys]
    in_specs += [pl.BlockSpec(w.shape, lambda i: (0, 0)) for w in ws]
    in_specs += [pl.BlockSpec((tm, d), lambda i: (i, 0))]
    return pl.pallas_call(
        functools.partial(_out_kernel, n_in=n_in),
        grid=(n_rows // tm,),
        in_specs=in_specs,
        out_specs=pl.BlockSpec((tm, d), lambda i: (i, 0)),
        out_shape=jax.ShapeDtypeStruct((n_rows, d), F32),
        compiler_params=_params(("arbitrary",)),
        name="out_proj_residual",
    )(*ys, *ws, x2d)


def _s5_prepare(lam_re, lam_im, log_dt, b_re, b_im, c_re, c_im):
    dt = jnp.exp(log_dt.astype(F32))[:, None]
    lr, li = lam_re.astype(F32), lam_im.astype(F32)
    mag = jnp.exp(lr * dt)
    ar, ai = mag * jnp.cos(li * dt), mag * jnp.sin(li * dt)
    den = lr * lr + li * li
    nr, ni = ar - 1.0, ai
    kr = (nr * lr + ni * li) / den
    ki = (ni * lr - nr * li) / den
    br, bi = b_re.astype(F32), b_im.astype(F32)
    bbr = kr[..., None] * br - ki[..., None] * bi
    bbi = kr[..., None] * bi + ki[..., None] * br
    nblk = S5_GROUPS // 16
    eye = jnp.eye(16, dtype=F32)

    def b_tiles(bb):
        t = bb.reshape(nblk, 16, S5_SLICES, 8, S5_GROUP)
        return jnp.einsum("ngkph,fg->knfhgp", t, eye).reshape(S5_SLICES, nblk, 256, LANES)

    def c_tiles(cc):
        t = cc.reshape(nblk, 16, S5_GROUP, S5_SLICES, 8)
        return jnp.einsum("nghkp,fg->nkgpfh", t, eye).reshape(nblk, S5_SLICES, LANES, 256)

    bmat = jnp.concatenate([b_tiles(bbr), b_tiles(bbi)], axis=-1).astype(BF16)
    cmat = jnp.concatenate([c_tiles(c_re.astype(F32)), c_tiles(-c_im.astype(F32))], axis=2)
    cmat = cmat.reshape(nblk, S5_SLICES * 256, 256).astype(BF16)

    def a_tiles(a):
        t = a.reshape(nblk, 16, S5_SLICES, 8).transpose(2, 0, 1, 3)
        return t.reshape(S5_SLICES, nblk * LANES)

    return bmat, cmat, a_tiles(ar), a_tiles(ai)


def _s5_kernel(u_ref, z_ref, bm_ref, cm_ref, ar_ref, ai_ref, d_ref, wg_ref, bg_ref,
               o_ref, s_ref, h_ref, *, chunk, pitch):
    nb = u_ref.shape[0]
    nblk = ar_ref.shape[1] // LANES

    @pl.when(pl.program_id(0) == 0)
    def _():
        h_ref[...] = jnp.zeros_like(h_ref)

    for b in range(nb):
        ub = u_ref[b].astype(BF16)
        for k in range(S5_SLICES):
            for n in range(nblk):
                x = jnp.dot(ub[:, n * 256:(n + 1) * 256], bm_ref[k, n],
                            preferred_element_type=F32)
                s_ref[b, 2 * n, k * pitch:k * pitch + chunk, :] = x[:, :LANES]
                s_ref[b, 2 * n + 1, k * pitch:k * pitch + chunk, :] = x[:, LANES:]

    ar = [ar_ref[:, n * LANES:(n + 1) * LANES] for n in range(nblk)]
    ai = [ai_ref[:, n * LANES:(n + 1) * LANES] for n in range(nblk)]

    def step(t, hs):
        out = []
        for b in range(nb):
            for n in range(nblk):
                hr, hi = hs[2 * (b * nblk + n)], hs[2 * (b * nblk + n) + 1]
                rows = pl.ds(t, S5_SLICES, stride=pitch)
                xr = s_ref[b, 2 * n, rows, :]
                xi = s_ref[b, 2 * n + 1, rows, :]
                nr = ar[n] * hr - ai[n] * hi + xr
                ni = ar[n] * hi + ai[n] * hr + xi
                s_ref[b, 2 * n, rows, :] = nr
                s_ref[b, 2 * n + 1, rows, :] = ni
                out += [nr, ni]
        return tuple(out)

    h0 = tuple(h_ref[b, s] for b in range(nb) for s in range(2 * nblk))
    hs = lax.fori_loop(0, chunk, step, h0, unroll=4)
    for b in range(nb):
        for s in range(2 * nblk):
            h_ref[b, s] = hs[b * 2 * nblk + s]

    for b in range(nb):
        ys = []
        for n in range(nblk):
            acc = None
            for k in range(S5_SLICES):
                rows = slice(k * pitch, k * pitch + chunk)
                hk = jnp.concatenate([s_ref[b, 2 * n, rows, :], s_ref[b, 2 * n + 1, rows, :]],
                                     axis=1).astype(BF16)
                part = jnp.dot(hk, cm_ref[n, k * 256:(k + 1) * 256, :],
                               preferred_element_type=F32)
                acc = part if acc is None else acc + part
            ys.append(acc)
        y = jnp.concatenate(ys, axis=1) + d_ref[...] * u_ref[b]
        y = jax.nn.gelu(y)
        gate = jax.nn.sigmoid(
            jnp.dot(y.astype(BF16), wg_ref[...], preferred_element_type=F32) + bg_ref[...])
        o_ref[b] = (y * gate * jax.nn.silu(z_ref[b])).astype(BF16)


def _s5_mixer(u, z, bmat, cmat, a_re, a_im, d, w_glu, b_glu, *, chunk):
    batch, seq, width = u.shape
    pitch = chunk + 8
    nblk = width // 256
    const = lambda shape: pl.BlockSpec(shape, lambda c: (0,) * len(shape))
    return pl.pallas_call(
        functools.partial(_s5_kernel, chunk=chunk, pitch=pitch),
        grid=(seq // chunk,),
        in_specs=[
            pl.BlockSpec((batch, chunk, width), lambda c: (0, c, 0)),
            pl.BlockSpec((batch, chunk, width), lambda c: (0, c, 0)),
            const(bmat.shape), const(cmat.shape), const(a_re.shape), const(a_im.shape),
            const((1, width)), const(w_glu.shape), const((1, width)),
        ],
        out_specs=pl.BlockSpec((batch, chunk, width), lambda c: (0, c, 0)),
        out_shape=jax.ShapeDtypeStruct((batch, seq, width), BF16),
        scratch_shapes=[
            pltpu.VMEM((batch, 2 * nblk, S5_SLICES * pitch, LANES), F32),
            pltpu.VMEM((batch, 2 * nblk, S5_SLICES, LANES), F32),
        ],
        compiler_params=_params(("arbitrary",)),
        name="s5_mixer",
    )(u, z, bmat, cmat, a_re, a_im, d.reshape(1, width), w_glu, b_glu.reshape(1, width))


def _split3(x):
    hi = x.astype(BF16).astype(F32)
    r = x - hi
    mid = r.astype(BF16).astype(F32)
    return hi, mid, r - mid


def _select_rows(index, values):
    out = jnp.zeros(index.shape, F32)
    for i, v in enumerate(values):
        out = jnp.where(index == i, v, out)
    return out


class _Chain:
    def __init__(self, kaug_blk, vt_blk, qaug_ref, stats, c, mask_off):
        self.kaug_blk, self.vt_blk, self.qaug_ref = kaug_blk, vt_blk, qaug_ref
        self.m_ref, self.acc_ref = stats
        self.lanes = slice(c * CHUNK, (c + 1) * CHUNK)
        self.mask_off = mask_off

    def scores(self):
        s = jnp.dot(self.kaug_blk, self.qaug_ref[:, self.lanes],
                    preferred_element_type=F32)
        if self.mask_off is not None:
            key = lax.broadcasted_iota(jnp.int32, s.shape, 0)
            qry = lax.broadcasted_iota(jnp.int32, s.shape, 1)
            s = jnp.where(key - qry <= self.mask_off, s, NEG)
        return s

    def softmax(self, t):
        m_prev = self.m_ref[:, self.lanes]
        m_new = jnp.maximum(m_prev, jnp.max(t, axis=0, keepdims=True))
        alpha = jnp.exp2(m_prev - m_new)
        p = jnp.exp2(t - m_new)
        self.m_ref[:, self.lanes] = m_new
        return p.astype(BF16), alpha

    def values(self, p, alpha):
        pv = jnp.dot(self.vt_blk, p, preferred_element_type=F32)
        self.acc_ref[:, self.lanes] = alpha * self.acc_ref[:, self.lanes] + pv


def _run_chains(chains):
    ready = [ch.scores() for ch in chains[:SCORES_AHEAD]]
    for i, ch in enumerate(chains):
        if i + SCORES_AHEAD < len(chains):
            ready.append(chains[i + SCORES_AHEAD].scores())
        p, alpha = ch.softmax(ready.pop(0))
        ch.values(p, alpha)


def _causal_sweep(qi, tq, tk, load, make):
    per, nch = tq // tk, tq // CHUNK

    def body(j, carry):
        chains = []
        for d in range(per):
            blk = load(j * per + d)
            for c in range(nch):
                chains += make(blk, c, None)
        _run_chains(chains)
        return carry

    lax.fori_loop(0, qi, body, 0)
    chains = []
    for d in range(per):
        blk = load(qi * per + d)
        for c in range(nch):
            off = c * CHUNK - d * tk
            if off < -(CHUNK - 1):
                continue
            chains += make(blk, c, None if off >= tk - 1 else off)
    _run_chains(chains)


def _init_stats(*pairs):
    for m_ref, a_ref in pairs:
        m_ref[...] = jnp.full_like(m_ref, NEG)
        a_ref[...] = jnp.zeros_like(a_ref)


def _values_with_ones(vt_blk):
    row = lax.broadcasted_iota(jnp.int32, (BF16_ROWS, vt_blk.shape[1]), 0)
    ones = jnp.where(row == 0, 1.0, 0.0).astype(BF16)
    return jnp.concatenate([vt_blk, ones], axis=0)


def _normalised(acc_ref):
    return acc_ref[:HEAD_W, :] * (1.0 / acc_ref[HEAD_W:HEAD_W + 1, :])


def _fox_kernel(qt_ref, k_ref, vt_ref, c_ref, z_ref, o_ref,
                kaug_ref, qaug_ref, m_ref, acc_ref, *, tq, tk):
    qi = pl.program_id(2)

    @pl.when(qi == 0)
    def _():
        lane = lax.broadcasted_iota(jnp.int32, (LANES, LANES), 1)

        def fill(r, carry):
            rows = pl.ds(pl.multiple_of(r * LANES, LANES), LANES)
            col = jnp.broadcast_to(c_ref[pl.ds(r, 1), :], (LANES, LANES)).T
            hi, mid, lo = _split3(col * (-LOG2E))
            ext = _select_rows(lane, (hi, mid, lo, 1.0, 1.0, 1.0))
            kaug_ref[rows, :LANES] = k_ref[rows, :]
            kaug_ref[rows, LANES:] = ext.astype(BF16)
            return carry

        lax.fori_loop(0, c_ref.shape[0], fill, 0, unroll=8)

    crow = c_ref[pl.ds(qi * (tq // LANES), 1), :]
    c0 = jnp.broadcast_to(crow[:, 0:1], (BF16_ROWS, tq)) * LOG2E
    hi, mid, lo = _split3(c0)
    row = lax.broadcasted_iota(jnp.int32, (BF16_ROWS, tq), 0)
    qaug_ref[:HEAD_W, :] = qt_ref[...]
    qaug_ref[HEAD_W:HEAD_W + BF16_ROWS, :] = _select_rows(row, (1.0, 1.0, 1.0, hi, mid, lo)).astype(BF16)
    qaug_ref[HEAD_W + BF16_ROWS:, :] = jnp.zeros((AUG - HEAD_W - BF16_ROWS, tq), BF16)
    _init_stats((m_ref, acc_ref))

    def load(kb):
        ks = pl.multiple_of(kb * tk, tk)
        return kaug_ref[pl.ds(ks, tk), :], _values_with_ones(vt_ref[kb])

    def make(blk, c, mask_off):
        return [_Chain(blk[0], blk[1], qaug_ref, (m_ref, acc_ref), c, mask_off)]

    _causal_sweep(qi, tq, tk, load, make)

    o = _normalised(acc_ref).T
    o_ref[...] = (o * jax.nn.silu(z_ref[...])).astype(BF16)


def _fox_attention(qt, k, vt, c2, z, *, tq, tk):
    batch, width, seq = qt.shape
    heads = width // HEAD_W
    return pl.pallas_call(
        functools.partial(_fox_kernel, tq=tq, tk=tk),
        grid=(batch, heads, seq // tq),
        in_specs=[
            pl.BlockSpec((None, HEAD_W, tq), lambda b, h, q: (b, h, q)),
            pl.BlockSpec((None, seq, HEAD_W), lambda b, h, q: (b, 0, h)),
            pl.BlockSpec((None, seq // tk, HEAD_W, tk), lambda b, h, q: (b, 0, h, 0)),
            pl.BlockSpec((None, None, seq // LANES, LANES), lambda b, h, q: (b, h, 0, 0)),
            pl.BlockSpec((None, tq, HEAD_W), lambda b, h, q: (b, q, h)),
        ],
        out_specs=pl.BlockSpec((None, tq, HEAD_W), lambda b, h, q: (b, q, h)),
        out_shape=jax.ShapeDtypeStruct((batch, seq, width), BF16),
        scratch_shapes=[
            pltpu.VMEM((seq, AUG), BF16), pltpu.VMEM((AUG, tq), BF16),
            pltpu.VMEM((1, tq), F32), pltpu.VMEM((ACC_ROWS, tq), F32),
        ],
        compiler_params=_params(("arbitrary", "arbitrary", "arbitrary")),
        name="fox_attention",
    )(qt, k, vt, c2, z)


def _diff_kernel(qt_ref, k_ref, vt_ref, hp_ref, og_ref, z_ref, o_ref,
                 kaug_ref, qa0_ref, qa1_ref, m0_ref, a0_ref, m1_ref, a1_ref,
                 *, tq, tk, out_scale):
    qi = pl.program_id(2)
    dh = qt_ref.shape[0] // 2
    slope2 = hp_ref[0:1, :] * LOG2E
    lam = hp_ref[1:2, 0:1]

    @pl.when(qi == 0)
    def _():
        lane = lax.broadcasted_iota(jnp.int32, (LANES, LANES), 1)
        key = lax.broadcasted_iota(jnp.int32, (LANES, LANES), 0)

        def fill(r, carry):
            rows = pl.ds(pl.multiple_of(r * LANES, LANES), LANES)
            base = (key * 0 + r * LANES).astype(F32)
            ext = _select_rows(lane, _split3(slope2 * base) + _split3(slope2 * key.astype(F32))
                               + (1.0, 1.0, 1.0))
            kaug_ref[rows, :LANES] = k_ref[rows, :]
            kaug_ref[rows, LANES:] = ext.astype(BF16)
            return carry

        lax.fori_loop(0, k_ref.shape[0] // LANES, fill, 0)

    row = lax.broadcasted_iota(jnp.int32, (BF16_ROWS, tq), 0)
    q0 = (row * 0 + qi * tq).astype(F32)
    slope_t = jnp.concatenate([slope2] * (tq // LANES), axis=1)
    ext = _select_rows(row, (1.0,) * 6 + _split3(-slope_t * q0)).astype(BF16)
    qt = qt_ref[...]
    first = lax.broadcasted_iota(jnp.int32, qt.shape, 0) < dh
    zero = jnp.zeros_like(qt)
    for qa_ref, comp in ((qa0_ref, jnp.where(first, qt, zero)), (qa1_ref, jnp.where(first, zero, qt))):
        qa_ref[:HEAD_W, :] = comp
        qa_ref[HEAD_W:HEAD_W + BF16_ROWS, :] = ext
        qa_ref[HEAD_W + BF16_ROWS:, :] = jnp.zeros((AUG - HEAD_W - BF16_ROWS, tq), BF16)
    _init_stats((m0_ref, a0_ref), (m1_ref, a1_ref))

    def load(kb):
        ks = pl.multiple_of(kb * tk, tk)
        return kaug_ref[pl.ds(ks, tk), :], _values_with_ones(vt_ref[kb])

    def make(blk, c, mask_off):
        return [_Chain(blk[0], blk[1], qa0_ref, (m0_ref, a0_ref), c, mask_off),
                _Chain(blk[0], blk[1], qa1_ref, (m1_ref, a1_ref), c, mask_off)]

    _causal_sweep(qi, tq, tk, load, make)

    ot = _normalised(a0_ref) - lam * _normalised(a1_ref)
    ms = jnp.mean(ot * ot, axis=0, keepdims=True)
    o = (ot * lax.rsqrt(ms + EPS)).T
    o = o * og_ref[...] * out_scale
    o_ref[...] = (o * jax.nn.silu(z_ref[...])).astype(BF16)


def _diff_attention(qt, k, vt, head_params, out_gain, z, *, tq, tk, out_scale):
    batch, width, seq = qt.shape
    heads = width // HEAD_W
    stat = lambda: pltpu.VMEM((1, tq), F32)
    accum = lambda: pltpu.VMEM((ACC_ROWS, tq), F32)
    return pl.pallas_call(
        functools.partial(_diff_kernel, tq=tq, tk=tk, out_scale=out_scale),
        grid=(batch, heads, seq // tq),
        in_specs=[
            pl.BlockSpec((None, HEAD_W, tq), lambda b, h, q: (b, h, q)),
            pl.BlockSpec((None, seq, HEAD_W), lambda b, h, q: (b, 0, h)),
            pl.BlockSpec((None, seq // tk, HEAD_W, tk), lambda b, h, q: (b, 0, h, 0)),
            pl.BlockSpec((None, 8, LANES), lambda b, h, q: (h, 0, 0)),
            pl.BlockSpec((1, HEAD_W), lambda b, h, q: (0, 0)),
            pl.BlockSpec((None, tq, HEAD_W), lambda b, h, q: (b, q, h)),
        ],
        out_specs=pl.BlockSpec((None, tq, HEAD_W), lambda b, h, q: (b, q, h)),
        out_shape=jax.ShapeDtypeStruct((batch, seq, width), BF16),
        scratch_shapes=[pltpu.VMEM((seq, AUG), BF16),
                        pltpu.VMEM((AUG, tq), BF16), pltpu.VMEM((AUG, tq), BF16),
                        stat(), accum(), stat(), accum()],
        compiler_params=_params(("arbitrary", "arbitrary", "arbitrary")),
        name="diff_attention",
    )(qt, k, vt, head_params, out_gain, z)


def _cumsum_kernel(lf_ref, o_ref, carry_ref, *, rows):
    @pl.when(pl.program_id(1) == 0)
    def _():
        carry_ref[...] = jnp.zeros_like(carry_ref)

    x = lf_ref[...]
    r = lax.broadcasted_iota(jnp.int32, (rows, rows), 0)
    c = lax.broadcasted_iota(jnp.int32, (rows, rows), 1)
    tri = (c <= r).astype(BF16)
    hi = x.astype(BF16)
    r1 = x - hi.astype(F32)
    mid = r1.astype(BF16)
    lo = (r1 - mid.astype(F32)).astype(BF16)
    cum = (jnp.dot(tri, hi, preferred_element_type=F32)
           + jnp.dot(tri, mid, preferred_element_type=F32)
           + jnp.dot(tri, lo, preferred_element_type=F32)) + carry_ref[...]
    carry_ref[...] = cum[rows - 1:rows, :]
    o_ref[...] = cum.T


def _cumsum_t(logf, *, rows):
    batch, seq, _ = logf.shape
    return pl.pallas_call(
        functools.partial(_cumsum_kernel, rows=rows),
        grid=(batch, seq // rows),
        in_specs=[pl.BlockSpec((None, rows, LANES), lambda b, i: (b, i, 0))],
        out_specs=pl.BlockSpec((None, LANES, rows), lambda b, i: (b, 0, i)),
        out_shape=jax.ShapeDtypeStruct((batch, LANES, seq), F32),
        scratch_shapes=[pltpu.VMEM((1, LANES), F32)],
        compiler_params=_params(("arbitrary", "arbitrary")),
        name="logf_cumsum",
    )(logf)


def _lane_col(v):
    return jnp.broadcast_to(v.astype(F32)[:, None], (v.shape[0], LANES))


def _alibi_slopes(n):
    return jnp.asarray([2.0 ** (-8.0 * (h + 1) / n) for h in range(n)], F32)


def _even_layer(x2d, layer, batch, seq, tiles, e_norm, e_w_in, e_w_out, lam_re, lam_im, log_dt,
                b_re, b_im, c_re, c_im, d, w_glu, b_glu, q_norm, k_norm, lq1, lk1, lq2, lk2,
                out_norm):
    tm, tq, tk, chunk = tiles
    s5w = S5_GROUPS * S5_GROUP
    lambda_init = 0.8 - 0.6 * math.exp(-0.3 * layer)
    qgain = _lane_col(jnp.tile(q_norm.astype(F32) * (DA_HEAD_DIM ** -0.5 * LOG2E), DA_HEADS))
    kgain = _lane_col(jnp.tile(k_norm, DA_HEADS))
    layout = (("f32", 0, 1), ("f32", 1, 1), ("qT", 2, 1), ("k", 3, 1), ("vT", 4, 1), ("f32", 5, 1))
    u, zs, qt, k, vt, zd = _project(
        x2d, e_norm, e_w_in.astype(BF16), qgain, kgain, batch=batch, seq=seq, tm=tm,
        layout=layout, head_dim=DA_HEAD_DIM)

    bmat, cmat, a_re, a_im = _s5_prepare(lam_re, lam_im, log_dt, b_re, b_im, c_re, c_im)
    y_s5 = _s5_mixer(u.reshape(batch, seq, s5w), zs.reshape(batch, seq, s5w), bmat, cmat,
                     a_re, a_im, d, w_glu.astype(BF16), b_glu, chunk=chunk)

    lam = (jnp.exp(jnp.sum(lq1.astype(F32) * lk1.astype(F32)))
           - jnp.exp(jnp.sum(lq2.astype(F32) * lk2.astype(F32))) + lambda_init)
    head_params = jnp.zeros((DA_HEADS, 8, LANES), F32)
    head_params = head_params.at[:, 0, :].set(_alibi_slopes(DA_HEADS)[:, None])
    head_params = head_params.at[:, 1, :].set(lam)
    y_da = _diff_attention(qt, k.reshape(batch, seq, -1), vt, head_params,
                           out_norm.astype(F32).reshape(1, HEAD_W),
                           zd.reshape(batch, seq, -1), tq=tq, tk=tk,
                           out_scale=1.0 - lambda_init)

    w_out = e_w_out.astype(BF16)
    return _out_project([y_s5.reshape(batch * seq, s5w), y_da.reshape(batch * seq, -1)],
                        [w_out[:s5w], w_out[s5w:]], x2d, tm=tm)


def _odd_layer(x2d, batch, seq, tiles, o_norm, o_w_in, o_b_f, o_w_out, q_norm, k_norm):
    tm, tq, tk, _ = tiles
    width = FOX_HEADS * HEAD_W
    nsec = width // SEC
    qgain = _lane_col(jnp.tile(q_norm.astype(F32) * (HEAD_W ** -0.5 * LOG2E), SEC // HEAD_W))
    kgain = _lane_col(jnp.tile(k_norm, SEC // HEAD_W))
    w = o_w_in.astype(BF16)
    wf = jnp.zeros((w.shape[0], LANES), BF16).at[:, :FOX_HEADS].set(w[:, 4 * width:])
    bf = jnp.zeros((1, LANES), F32).at[0, :FOX_HEADS].set(o_b_f.astype(F32))
    layout = (("qT", 0, nsec), ("k", nsec, nsec), ("vT", 2 * nsec, nsec), ("f32", 3 * nsec, nsec))
    qt, k, vt, z, logf = _project(
        x2d, o_norm, w[:, :4 * width], qgain, kgain, batch=batch, seq=seq, tm=tm,
        layout=layout, head_dim=HEAD_W, wf=wf, bf=bf)
    ct = _cumsum_t(logf.reshape(batch, seq, LANES), rows=tm)
    c2 = ct[:, :FOX_HEADS].reshape(batch, FOX_HEADS, seq // LANES, LANES)
    y = _fox_attention(qt, k.reshape(batch, seq, width), vt, c2, z.reshape(batch, seq, width),
                       tq=tq, tk=tk)
    return _out_project([y.reshape(batch * seq, width)], [o_w_out.astype(BF16)], x2d, tm=tm)


def _tiles(seq):
    tm = min(512, seq)
    return tm, min(4096, seq), tm, min(128, seq)


def kernel(x, e_norm, e_w_in, e_w_out, s5_lambda_re, s5_lambda_im, s5_log_dt, s5_b_re, s5_b_im, s5_c_re, s5_c_im, s5_d, s5_w_glu, s5_b_glu, da_q_norm, da_k_norm, da_lambda_q1, da_lambda_k1, da_lambda_q2, da_lambda_k2, da_out_norm, o_norm, o_w_in, o_b_f, o_w_out, fox_q_norm, fox_k_norm):
    batch, seq, d_model = x.shape
    tiles = _tiles(seq)
    depth = e_norm.shape[0] + o_norm.shape[0]
    x2d = x.reshape(batch * seq, d_model)
    for layer in range(depth):
        j = layer // 2
        if layer % 2 == 0:
            x2d = _even_layer(
                x2d, layer, batch, seq, tiles, e_norm[j], e_w_in[j], e_w_out[j],
                s5_lambda_re[j], s5_lambda_im[j], s5_log_dt[j], s5_b_re[j], s5_b_im[j],
                s5_c_re[j], s5_c_im[j], s5_d[j], s5_w_glu[j], s5_b_glu[j],
                da_q_norm[j], da_k_norm[j], da_lambda_q1[j], da_lambda_k1[j],
                da_lambda_q2[j], da_lambda_k2[j], da_out_norm[j])
        else:
            x2d = _odd_layer(x2d, batch, seq, tiles, o_norm[j], o_w_in[j], o_b_f[j], o_w_out[j],
                             fox_q_norm[j], fox_k_norm[j])
    return x2d.reshape(batch, seq, d_model)
```

```python
import functools
import math

import jax
import jax.numpy as jnp
from jax import lax
from jax.experimental import pallas as pl
from jax.experimental.pallas import tpu as pltpu

F32 = jnp.float32
BF16 = jnp.bfloat16
EPS = 1e-6
NEG = -1e30
LOG2E = math.log2(math.e)

LANES = 128
BF16_ROWS = 16
AUG = 256
CHUNK = 256
SCORES_AHEAD = 3
SEC = 1024
VMEM_LIMIT = 52 * 1024 * 1024

S5_GROUPS = 64
S5_GROUP = 16
S5_STATE = 64
S5_SLICES = 8
DA_HEADS = 8
DA_HEAD_DIM = 64
FOX_HEADS = 16
HEAD_W = 128
ACC_ROWS = HEAD_W + BF16_ROWS


def _params(sem):
    return pltpu.CompilerParams(dimension_semantics=sem, vmem_limit_bytes=VMEM_LIMIT)


def _head_norm_t(acc_t, gain_ref, head_dim):
    tm = acc_t.shape[1]
    r = acc_t.reshape(SEC // head_dim, head_dim, tm)
    ms = jnp.mean(r * r, axis=1, keepdims=True)
    n = (r * lax.rsqrt(ms + EPS)).reshape(SEC, tm)
    g = gain_ref[...]
    return jnp.concatenate(
        [n[:, c * LANES:(c + 1) * LANES] * g for c in range(tm // LANES)], axis=1)


def _proj_kernel(*refs, layout, head_dim, has_logf):
    it = iter(refs)
    x_ref, g_ref, w_ref = next(it), next(it), next(it)
    if has_logf:
        wf_ref, bf_ref = next(it), next(it)
    qg_ref, kg_ref = next(it), next(it)
    out_refs = [next(it) for _ in layout]
    if has_logf:
        logf_ref = next(it)
    hs_ref = next(it)

    j = pl.program_id(1)

    @pl.when(j == 0)
    def _():
        x = x_ref[...]
        ms = jnp.mean(x * x, axis=-1, keepdims=True)
        hs_ref[...] = (x * lax.rsqrt(ms + EPS) * g_ref[...]).astype(BF16)
        if has_logf:
            f = jnp.dot(hs_ref[...], wf_ref[...], preferred_element_type=F32) + bf_ref[...]
            logf_ref[...] = jax.nn.log_sigmoid(f)

    acc = jnp.dot(hs_ref[...], w_ref[...], preferred_element_type=F32)

    for (kind, j0, cnt), o_ref in zip(layout, out_refs):
        @pl.when((j >= j0) & (j < j0 + cnt))
        def _(kind=kind, o_ref=o_ref):
            if kind == "f32":
                o_ref[...] = acc
            elif kind == "vT":
                o_ref[...] = acc.T.astype(BF16)
            elif kind == "qT":
                o_ref[...] = _head_norm_t(acc.T, qg_ref, head_dim).astype(BF16)
            else:
                o_ref[...] = _head_norm_t(acc.T, kg_ref, head_dim).T.astype(BF16)


def _project(x2d, norm_g, w, qgain, kgain, *, batch, seq, tm, layout, head_dim, wf=None, bf=None):
    n_rows, d = x2d.shape
    nsec = w.shape[1] // SEC
    nt = seq // tm
    has_logf = wf is not None

    def sec(j, j0, cnt):
        return jnp.clip(j - j0, 0, cnt - 1)

    in_specs = [
        pl.BlockSpec((tm, d), lambda i, j: (i, 0)),
        pl.BlockSpec((1, d), lambda i, j: (0, 0)),
        pl.BlockSpec((d, SEC), lambda i, j: (0, j)),
    ]
    args = [x2d, norm_g.reshape(1, d), w]
    if has_logf:
        in_specs += [pl.BlockSpec((d, LANES), lambda i, j: (0, 0)),
                     pl.BlockSpec((1, LANES), lambda i, j: (0, 0))]
        args += [wf, bf]
    in_specs += [pl.BlockSpec((SEC, LANES), lambda i, j: (0, 0)),
                 pl.BlockSpec((SEC, LANES), lambda i, j: (0, 0))]
    args += [qgain, kgain]

    out_shapes, out_specs = [], []
    for kind, j0, cnt in layout:
        if kind == "f32" or kind == "k":
            dt = F32 if kind == "f32" else BF16
            out_shapes.append(jax.ShapeDtypeStruct((n_rows, cnt * SEC), dt))
            out_specs.append(pl.BlockSpec(
                (tm, SEC), lambda i, j, j0=j0, cnt=cnt: (i, sec(j, j0, cnt))))
        elif kind == "qT":
            out_shapes.append(jax.ShapeDtypeStruct((batch, cnt * SEC, seq), BF16))
            out_specs.append(pl.BlockSpec(
                (None, SEC, tm), lambda i, j, j0=j0, cnt=cnt: (i // nt, sec(j, j0, cnt), i % nt)))
        else:
            out_shapes.append(jax.ShapeDtypeStruct((batch, nt, cnt * SEC, tm), BF16))
            out_specs.append(pl.BlockSpec(
                (None, None, SEC, tm),
                lambda i, j, j0=j0, cnt=cnt: (i // nt, i % nt, sec(j, j0, cnt), 0)))
    if has_logf:
        out_shapes.append(jax.ShapeDtypeStruct((n_rows, LANES), F32))
        out_specs.append(pl.BlockSpec((tm, LANES), lambda i, j: (i, 0)))

    return pl.pallas_call(
        functools.partial(_proj_kernel, layout=layout, head_dim=head_dim, has_logf=has_logf),
        grid=(n_rows // tm, nsec),
        in_specs=in_specs,
        out_specs=out_specs,
        out_shape=out_shapes,
        scratch_shapes=[pltpu.VMEM((tm, d), BF16)],
        compiler_params=_params(("arbitrary", "arbitrary")),
        name="rms_in_proj",
    )(*args)


def _out_kernel(*refs, n_in):
    ys, ws = refs[:n_in], refs[n_in:2 * n_in]
    x_ref, o_ref = refs[2 * n_in], refs[2 * n_in + 1]
    acc = x_ref[...]
    for y_ref, w_ref in zip(ys, ws):
        acc = acc + jnp.dot(y_ref[...], w_ref[...], preferred_element_type=F32)
    o_ref[...] = acc


def _out_project(ys, ws, x2d, *, tm):
    n_rows, d = x2d.shape
    n_in = len(ys)
    in_specs = [pl.BlockSpec((tm, y.shape[1]), lambda i: (i, 0)) for y in ys]
    in_specs += [pl.BlockSpec(w.shape, lambda i: (0, 0)) for w in ws]
    in_specs += [pl.BlockSpec((tm, d), lambda i: (i, 0))]
    return pl.pallas_call(
        functools.partial(_out_kernel, n_in=n_in),
        grid=(n_rows // tm,),
        in_specs=in_specs,
        out_specs=pl.BlockSpec((tm, d), lambda i: (i, 0)),
        out_shape=jax.ShapeDtypeStruct((n_rows, d), F32),
        compiler_params=_params(("arbitrary",)),
        name="out_proj_residual",
    )(*ys, *ws, x2d)


def _s5_prepare(lam_re, lam_im, log_dt, b_re, b_im, c_re, c_im):
    dt = jnp.exp(log_dt.astype(F32))[:, None]
    lr, li = lam_re.astype(F32), lam_im.astype(F32)
    mag = jnp.exp(lr * dt)
    ar, ai = mag * jnp.cos(li * dt), mag * jnp.sin(li * dt)
    den = lr * lr + li * li
    nr, ni = ar - 1.0, ai
    kr = (nr * lr + ni * li) / den
    ki = (ni * lr - nr * li) / den
    br, bi = b_re.astype(F32), b_im.astype(F32)
    bbr = kr[..., None] * br - ki[..., None] * bi
    bbi = kr[..., None] * bi + ki[..., None] * br
    nblk = S5_GROUPS // 16
    eye = jnp.eye(16, dtype=F32)

    def b_tiles(bb):
        t = bb.reshape(nblk, 16, S5_SLICES, 8, S5_GROUP)
        return jnp.einsum("ngkph,fg->knfhgp", t, eye).reshape(S5_SLICES, nblk, 256, LANES)

    def c_tiles(cc):
        t = cc.reshape(nblk, 16, S5_GROUP, S5_SLICES, 8)
        return jnp.einsum("nghkp,fg->nkgpfh", t, eye).reshape(nblk, S5_SLICES, LANES, 256)

    bmat = jnp.concatenate([b_tiles(bbr), b_tiles(bbi)], axis=-1).astype(BF16)
    cmat = jnp.concatenate([c_tiles(c_re.astype(F32)), c_tiles(-c_im.astype(F32))], axis=2)
    cmat = cmat.reshape(nblk, S5_SLICES * 256, 256).astype(BF16)

    def a_tiles(a):
        t = a.reshape(nblk, 16, S5_SLICES, 8).transpose(2, 0, 1, 3)
        return t.reshape(S5_SLICES, nblk * LANES)

    return bmat, cmat, a_tiles(ar), a_tiles(ai)


def _s5_kernel(u_ref, z_ref, bm_ref, cm_ref, ar_ref, ai_ref, d_ref, wg_ref, bg_ref,
               o_ref, s_ref, h_ref, *, chunk, pitch):
    nb = u_ref.shape[0]
    nblk = ar_ref.shape[1] // LANES

    @pl.when(pl.program_id(0) == 0)
    def _():
        h_ref[...] = jnp.zeros_like(h_ref)

    for b in range(nb):
        ub = u_ref[b].astype(BF16)
        for k in range(S5_SLICES):
            for n in range(nblk):
                x = jnp.dot(ub[:, n * 256:(n + 1) * 256], bm_ref[k, n],
                            preferred_element_type=F32)
                s_ref[b, 2 * n, k * pitch:k * pitch + chunk, :] = x[:, :LANES]
                s_ref[b, 2 * n + 1, k * pitch:k * pitch + chunk, :] = x[:, LANES:]

    ar = [ar_ref[:, n * LANES:(n + 1) * LANES] for n in range(nblk)]
    ai = [ai_ref[:, n * LANES:(n + 1) * LANES] for n in range(nblk)]

    def step(t, hs):
        out = []
        for b in range(nb):
            for n in range(nblk):
                hr, hi = hs[2 * (b * nblk + n)], hs[2 * (b * nblk + n) + 1]
                rows = pl.ds(t, S5_SLICES, stride=pitch)
                xr = s_ref[b, 2 * n, rows, :]
                xi = s_ref[b, 2 * n + 1, rows, :]
                nr = ar[n] * hr - ai[n] * hi + xr
                ni = ar[n] * hi + ai[n] * hr + xi
                s_ref[b, 2 * n, rows, :] = nr
                s_ref[b, 2 * n + 1, rows, :] = ni
                out += [nr, ni]
        return tuple(out)

    h0 = tuple(h_ref[b, s] for b in range(nb) for s in range(2 * nblk))
    hs = lax.fori_loop(0, chunk, step, h0, unroll=8)
    for b in range(nb):
        for s in range(2 * nblk):
            h_ref[b, s] = hs[b * 2 * nblk + s]

    for b in range(nb):
        ys = []
        for n in range(nblk):
            acc = None
            for k in range(S5_SLICES):
                rows = slice(k * pitch, k * pitch + chunk)
                hk = jnp.concatenate([s_ref[b, 2 * n, rows, :], s_ref[b, 2 * n + 1, rows, :]],
                                     axis=1).astype(BF16)
                part = jnp.dot(hk, cm_ref[n, k * 256:(k + 1) * 256, :],
                               preferred_element_type=F32)
                acc = part if acc is None else acc + part
            ys.append(acc)
        y = jnp.concatenate(ys, axis=1) + d_ref[...] * u_ref[b]
        y = jax.nn.gelu(y)
        gate = jax.nn.sigmoid(
            jnp.dot(y.astype(BF16), wg_ref[...], preferred_element_type=F32) + bg_ref[...])
        o_ref[b] = (y * gate * jax.nn.silu(z_ref[b])).astype(BF16)


def _s5_mixer(u, z, bmat, cmat, a_re, a_im, d, w_glu, b_glu, *, chunk):
    batch, seq, width = u.shape
    pitch = chunk + 8
    nblk = width // 256
    const = lambda shape: pl.BlockSpec(shape, lambda c: (0,) * len(shape))
    return pl.pallas_call(
        functools.partial(_s5_kernel, chunk=chunk, pitch=pitch),
        grid=(seq // chunk,),
        in_specs=[
            pl.BlockSpec((batch, chunk, width), lambda c: (0, c, 0)),
            pl.BlockSpec((batch, chunk, width), lambda c: (0, c, 0)),
            const(bmat.shape), const(cmat.shape), const(a_re.shape), const(a_im.shape),
            const((1, width)), const(w_glu.shape), const((1, width)),
        ],
        out_specs=pl.BlockSpec((batch, chunk, width), lambda c: (0, c, 0)),
        out_shape=jax.ShapeDtypeStruct((batch, seq, width), BF16),
        scratch_shapes=[
            pltpu.VMEM((batch, 2 * nblk, S5_SLICES * pitch, LANES), F32),
            pltpu.VMEM((batch, 2 * nblk, S5_SLICES, LANES), F32),
        ],
        compiler_params=_params(("arbitrary",)),
        name="s5_mixer",
    )(u, z, bmat, cmat, a_re, a_im, d.reshape(1, width), w_glu, b_glu.reshape(1, width))


def _split3(x):
    hi = x.astype(BF16).astype(F32)
    r = x - hi
    mid = r.astype(BF16).astype(F32)
    return hi, mid, r - mid


def _select_rows(index, values):
    out = jnp.zeros(index.shape, F32)
    for i, v in enumerate(values):
        out = jnp.where(index == i, v, out)
    return out


class _Chain:
    def __init__(self, kaug_blk, vt_blk, qaug_ref, stats, c, mask_off):
        self.kaug_blk, self.vt_blk, self.qaug_ref = kaug_blk, vt_blk, qaug_ref
        self.m_ref, self.acc_ref = stats
        self.lanes = slice(c * CHUNK, (c + 1) * CHUNK)
        self.mask_off = mask_off

    def scores(self):
        s = jnp.dot(self.kaug_blk, self.qaug_ref[:, self.lanes],
                    preferred_element_type=F32)
        if self.mask_off is not None:
            key = lax.broadcasted_iota(jnp.int32, s.shape, 0)
            qry = lax.broadcasted_iota(jnp.int32, s.shape, 1)
            s = jnp.where(key - qry <= self.mask_off, s, NEG)
        return s

    def softmax(self, t):
        m_prev = self.m_ref[:, self.lanes]
        m_new = jnp.maximum(m_prev, jnp.max(t, axis=0, keepdims=True))
        alpha = jnp.exp2(m_prev - m_new)
        p = jnp.exp2(t - m_new)
        self.m_ref[:, self.lanes] = m_new
        return p.astype(BF16), alpha

    def values(self, p, alpha):
        pv = jnp.dot(self.vt_blk, p, preferred_element_type=F32)
        self.acc_ref[:, self.lanes] = alpha * self.acc_ref[:, self.lanes] + pv


def _run_chains(chains):
    ready = [ch.scores() for ch in chains[:SCORES_AHEAD]]
    for i, ch in enumerate(chains):
        if i + SCORES_AHEAD < len(chains):
            ready.append(chains[i + SCORES_AHEAD].scores())
        p, alpha = ch.softmax(ready.pop(0))
        ch.values(p, alpha)


def _causal_sweep(qi, tq, tk, load, make):
    per, nch = tq // tk, tq // CHUNK

    def body(j, carry):
        chains = []
        for d in range(per):
            blk = load(j * per + d)
            for c in range(nch):
                chains += make(blk, c, None)
        _run_chains(chains)
        return carry

    lax.fori_loop(0, qi, body, 0)
    chains = []
    for d in range(per):
        blk = load(qi * per + d)
        for c in range(nch):
            off = c * CHUNK - d * tk
            if off < -(CHUNK - 1):
                continue
            chains += make(blk, c, None if off >= tk - 1 else off)
    _run_chains(chains)


def _init_stats(*pairs):
    for m_ref, a_ref in pairs:
        m_ref[...] = jnp.full_like(m_ref, NEG)
        a_ref[...] = jnp.zeros_like(a_ref)


def _values_with_ones(vt_blk):
    row = lax.broadcasted_iota(jnp.int32, (BF16_ROWS, vt_blk.shape[1]), 0)
    ones = jnp.where(row == 0, 1.0, 0.0).astype(BF16)
    return jnp.concatenate([vt_blk, ones], axis=0)


def _normalised(acc_ref):
    return acc_ref[:HEAD_W, :] * (1.0 / acc_ref[HEAD_W:HEAD_W + 1, :])


def _fox_kernel(qt_ref, k_ref, vt_ref, c_ref, z_ref, o_ref,
                kaug_ref, qaug_ref, m_ref, acc_ref, *, tq, tk):
    qi = pl.program_id(2)

    @pl.when(qi == 0)
    def _():
        lane = lax.broadcasted_iota(jnp.int32, (LANES, LANES), 1)

        def fill(r, carry):
            rows = pl.ds(pl.multiple_of(r * LANES, LANES), LANES)
            col = jnp.broadcast_to(c_ref[pl.ds(r, 1), :], (LANES, LANES)).T
            hi, mid, lo = _split3(col * (-LOG2E))
            ext = _select_rows(lane, (hi, mid, lo, 1.0, 1.0, 1.0))
            kaug_ref[rows, :LANES] = k_ref[rows, :]
            kaug_ref[rows, LANES:] = ext.astype(BF16)
            return carry

        lax.fori_loop(0, c_ref.shape[0], fill, 0, unroll=8)

    crow = c_ref[pl.ds(qi * (tq // LANES), 1), :]
    c0 = jnp.broadcast_to(crow[:, 0:1], (BF16_ROWS, tq)) * LOG2E
    hi, mid, lo = _split3(c0)
    row = lax.broadcasted_iota(jnp.int32, (BF16_ROWS, tq), 0)
    qaug_ref[:HEAD_W, :] = qt_ref[...]
    qaug_ref[HEAD_W:HEAD_W + BF16_ROWS, :] = _select_rows(row, (1.0, 1.0, 1.0, hi, mid, lo)).astype(BF16)
    qaug_ref[HEAD_W + BF16_ROWS:, :] = jnp.zeros((AUG - HEAD_W - BF16_ROWS, tq), BF16)
    _init_stats((m_ref, acc_ref))

    def load(kb):
        ks = pl.multiple_of(kb * tk, tk)
        return kaug_ref[pl.ds(ks, tk), :], _values_with_ones(vt_ref[kb])

    def make(blk, c, mask_off):
        return [_Chain(blk[0], blk[1], qaug_ref, (m_ref, acc_ref), c, mask_off)]

    _causal_sweep(qi, tq, tk, load, make)

    o = _normalised(acc_ref).T
    o_ref[...] = (o * jax.nn.silu(z_ref[...])).astype(BF16)


def _fox_attention(qt, k, vt, c2, z, *, tq, tk):
    batch, width, seq = qt.shape
    heads = width // HEAD_W
    return pl.pallas_call(
        functools.partial(_fox_kernel, tq=tq, tk=tk),
        grid=(batch, heads, seq // tq),
        in_specs=[
            pl.BlockSpec((None, HEAD_W, tq), lambda b, h, q: (b, h, q)),
            pl.BlockSpec((None, seq, HEAD_W), lambda b, h, q: (b, 0, h)),
            pl.BlockSpec((None, seq // tk, HEAD_W, tk), lambda b, h, q: (b, 0, h, 0)),
            pl.BlockSpec((None, None, seq // LANES, LANES), lambda b, h, q: (b, h, 0, 0)),
            pl.BlockSpec((None, tq, HEAD_W), lambda b, h, q: (b, q, h)),
        ],
        out_specs=pl.BlockSpec((None, tq, HEAD_W), lambda b, h, q: (b, q, h)),
        out_shape=jax.ShapeDtypeStruct((batch, seq, width), BF16),
        scratch_shapes=[
            pltpu.VMEM((seq, AUG), BF16), pltpu.VMEM((AUG, tq), BF16),
            pltpu.VMEM((1, tq), F32), pltpu.VMEM((ACC_ROWS, tq), F32),
        ],
        compiler_params=_params(("arbitrary", "arbitrary", "arbitrary")),
        name="fox_attention",
    )(qt, k, vt, c2, z)


def _diff_kernel(qt_ref, k_ref, vt_ref, hp_ref, og_ref, z_ref, o_ref,
                 kaug_ref, qa0_ref, qa1_ref, m0_ref, a0_ref, m1_ref, a1_ref,
                 *, tq, tk, out_scale):
    qi = pl.program_id(2)
    dh = qt_ref.shape[0] // 2
    slope2 = hp_ref[0:1, :] * LOG2E
    lam = hp_ref[1:2, 0:1]

    @pl.when(qi == 0)
    def _():
        lane = lax.broadcasted_iota(jnp.int32, (LANES, LANES), 1)
        key = lax.broadcasted_iota(jnp.int32, (LANES, LANES), 0)

        def fill(r, carry):
            rows = pl.ds(pl.multiple_of(r * LANES, LANES), LANES)
            base = (key * 0 + r * LANES).astype(F32)
            ext = _select_rows(lane, _split3(slope2 * base) + _split3(slope2 * key.astype(F32))
                               + (1.0, 1.0, 1.0))
            kaug_ref[rows, :LANES] = k_ref[rows, :]
            kaug_ref[rows, LANES:] = ext.astype(BF16)
            return carry

        lax.fori_loop(0, k_ref.shape[0] // LANES, fill, 0)

    row = lax.broadcasted_iota(jnp.int32, (BF16_ROWS, tq), 0)
    q0 = (row * 0 + qi * tq).astype(F32)
    slope_t = jnp.concatenate([slope2] * (tq // LANES), axis=1)
    ext = _select_rows(row, (1.0,) * 6 + _split3(-slope_t * q0)).astype(BF16)
    qt = qt_ref[...]
    first = lax.broadcasted_iota(jnp.int32, qt.shape, 0) < dh
    zero = jnp.zeros_like(qt)
    for qa_ref, comp in ((qa0_ref, jnp.where(first, qt, zero)), (qa1_ref, jnp.where(first, zero, qt))):
        qa_ref[:HEAD_W, :] = comp
        qa_ref[HEAD_W:HEAD_W + BF16_ROWS, :] = ext
        qa_ref[HEAD_W + BF16_ROWS:, :] = jnp.zeros((AUG - HEAD_W - BF16_ROWS, tq), BF16)
    _init_stats((m0_ref, a0_ref), (m1_ref, a1_ref))

    def load(kb):
        ks = pl.multiple_of(kb * tk, tk)
        return kaug_ref[pl.ds(ks, tk), :], _values_with_ones(vt_ref[kb])

    def make(blk, c, mask_off):
        return [_Chain(blk[0], blk[1], qa0_ref, (m0_ref, a0_ref), c, mask_off),
                _Chain(blk[0], blk[1], qa1_ref, (m1_ref, a1_ref), c, mask_off)]

    _causal_sweep(qi, tq, tk, load, make)

    ot = _normalised(a0_ref) - lam * _normalised(a1_ref)
    ms = jnp.mean(ot * ot, axis=0, keepdims=True)
    o = (ot * lax.rsqrt(ms + EPS)).T
    o = o * og_ref[...] * out_scale
    o_ref[...] = (o * jax.nn.silu(z_ref[...])).astype(BF16)


def _diff_attention(qt, k, vt, head_params, out_gain, z, *, tq, tk, out_scale):
    batch, width, seq = qt.shape
    heads = width // HEAD_W
    stat = lambda: pltpu.VMEM((1, tq), F32)
    accum = lambda: pltpu.VMEM((ACC_ROWS, tq), F32)
    return pl.pallas_call(
        functools.partial(_diff_kernel, tq=tq, tk=tk, out_scale=out_scale),
        grid=(batch, heads, seq // tq),
        in_specs=[
            pl.BlockSpec((None, HEAD_W, tq), lambda b, h, q: (b, h, q)),
            pl.BlockSpec((None, seq, HEAD_W), lambda b, h, q: (b, 0, h)),
            pl.BlockSpec((None, seq // tk, HEAD_W, tk), lambda b, h, q: (b, 0, h, 0)),
            pl.BlockSpec((None, 8, LANES), lambda b, h, q: (h, 0, 0)),
            pl.BlockSpec((1, HEAD_W), lambda b, h, q: (0, 0)),
            pl.BlockSpec((None, tq, HEAD_W), lambda b, h, q: (b, q, h)),
        ],
        out_specs=pl.BlockSpec((None, tq, HEAD_W), lambda b, h, q: (b, q, h)),
        out_shape=jax.ShapeDtypeStruct((batch, seq, width), BF16),
        scratch_shapes=[pltpu.VMEM((seq, AUG), BF16),
                        pltpu.VMEM((AUG, tq), BF16), pltpu.VMEM((AUG, tq), BF16),
                        stat(), accum(), stat(), accum()],
        compiler_params=_params(("arbitrary", "arbitrary", "arbitrary")),
        name="diff_attention",
    )(qt, k, vt, head_params, out_gain, z)


def _cumsum_kernel(lf_ref, o_ref, carry_ref, *, rows):
    @pl.when(pl.program_id(1) == 0)
    def _():
        carry_ref[...] = jnp.zeros_like(carry_ref)

    x = lf_ref[...]
    r = lax.broadcasted_iota(jnp.int32, (rows, rows), 0)
    c = lax.broadcasted_iota(jnp.int32, (rows, rows), 1)
    tri = (c <= r).astype(BF16)
    hi = x.astype(BF16)
    r1 = x - hi.astype(F32)
    mid = r1.astype(BF16)
    lo = (r1 - mid.astype(F32)).astype(BF16)
    cum = (jnp.dot(tri, hi, preferred_element_type=F32)
           + jnp.dot(tri, mid, preferred_element_type=F32)
           + jnp.dot(tri, lo, preferred_element_type=F32)) + carry_ref[...]
    carry_ref[...] = cum[rows - 1:rows, :]
    o_ref[...] = cum.T


def _cumsum_t(logf, *, rows):
    batch, seq, _ = logf.shape
    return pl.pallas_call(
        functools.partial(_cumsum_kernel, rows=rows),
        grid=(batch, seq // rows),
        in_specs=[pl.BlockSpec((None, rows, LANES), lambda b, i: (b, i, 0))],
        out_specs=pl.BlockSpec((None, LANES, rows), lambda b, i: (b, 0, i)),
        out_shape=jax.ShapeDtypeStruct((batch, LANES, seq), F32),
        scratch_shapes=[pltpu.VMEM((1, LANES), F32)],
        compiler_params=_params(("arbitrary", "arbitrary")),
        name="logf_cumsum",
    )(logf)


def _lane_col(v):
    return jnp.broadcast_to(v.astype(F32)[:, None], (v.shape[0], LANES))


def _alibi_slopes(n):
    return jnp.asarray([2.0 ** (-8.0 * (h + 1) / n) for h in range(n)], F32)


def _even_layer(x2d, layer, batch, seq, tiles, e_norm, e_w_in, e_w_out, lam_re, lam_im, log_dt,
                b_re, b_im, c_re, c_im, d, w_glu, b_glu, q_norm, k_norm, lq1, lk1, lq2, lk2,
                out_norm):
    tm, tq, tk, chunk = tiles
    s5w = S5_GROUPS * S5_GROUP
    lambda_init = 0.8 - 0.6 * math.exp(-0.3 * layer)
    qgain = _lane_col(jnp.tile(q_norm.astype(F32) * (DA_HEAD_DIM ** -0.5 * LOG2E), DA_HEADS))
    kgain = _lane_col(jnp.tile(k_norm, DA_HEADS))
    layout = (("f32", 0, 1), ("f32", 1, 1), ("qT", 2, 1), ("k", 3, 1), ("vT", 4, 1), ("f32", 5, 1))
    u, zs, qt, k, vt, zd = _project(
        x2d, e_norm, e_w_in.astype(BF16), qgain, kgain, batch=batch, seq=seq, tm=tm,
        layout=layout, head_dim=DA_HEAD_DIM)

    bmat, cmat, a_re, a_im = _s5_prepare(lam_re, lam_im, log_dt, b_re, b_im, c_re, c_im)
    y_s5 = _s5_mixer(u.reshape(batch, seq, s5w), zs.reshape(batch, seq, s5w), bmat, cmat,
                     a_re, a_im, d, w_glu.astype(BF16), b_glu, chunk=chunk)

    lam = (jnp.exp(jnp.sum(lq1.astype(F32) * lk1.astype(F32)))
           - jnp.exp(jnp.sum(lq2.astype(F32) * lk2.astype(F32))) + lambda_init)
    head_params = jnp.zeros((DA_HEADS, 8, LANES), F32)
    head_params = head_params.at[:, 0, :].set(_alibi_slopes(DA_HEADS)[:, None])
    head_params = head_params.at[:, 1, :].set(lam)
    y_da = _diff_attention(qt, k.reshape(batch, seq, -1), vt, head_params,
                           out_norm.astype(F32).reshape(1, HEAD_W),
                           zd.reshape(batch, seq, -1), tq=tq, tk=tk,
                           out_scale=1.0 - lambda_init)

    w_out = e_w_out.astype(BF16)
    return _out_project([y_s5.reshape(batch * seq, s5w), y_da.reshape(batch * seq, -1)],
                        [w_out[:s5w], w_out[s5w:]], x2d, tm=tm)


def _odd_layer(x2d, batch, seq, tiles, o_norm, o_w_in, o_b_f, o_w_out, q_norm, k_norm):
    tm, tq, tk, _ = tiles
    width = FOX_HEADS * HEAD_W
    nsec = width // SEC
    qgain = _lane_col(jnp.tile(q_norm.astype(F32) * (HEAD_W ** -0.5 * LOG2E), SEC // HEAD_W))
    kgain = _lane_col(jnp.tile(k_norm, SEC // HEAD_W))
    w = o_w_in.astype(BF16)
    wf = jnp.zeros((w.shape[0], LANES), BF16).at[:, :FOX_HEADS].set(w[:, 4 * width:])
    bf = jnp.zeros((1, LANES), F32).at[0, :FOX_HEADS].set(o_b_f.astype(F32))
    layout = (("qT", 0, nsec), ("k", nsec, nsec), ("vT", 2 * nsec, nsec), ("f32", 3 * nsec, nsec))
    qt, k, vt, z, logf = _project(
        x2d, o_norm, w[:, :4 * width], qgain, kgain, batch=batch, seq=seq, tm=tm,
        layout=layout, head_dim=HEAD_W, wf=wf, bf=bf)
    ct = _cumsum_t(logf.reshape(batch, seq, LANES), rows=tm)
    c2 = ct[:, :FOX_HEADS].reshape(batch, FOX_HEADS, seq // LANES, LANES)
    y = _fox_attention(qt, k.reshape(batch, seq, width), vt, c2, z.reshape(batch, seq, width),
                       tq=min(2 * tq, seq), tk=tk)
    return _out_project([y.reshape(batch * seq, width)], [o_w_out.astype(BF16)], x2d, tm=tm)


def _tiles(seq):
    tm = min(512, seq)
    return tm, min(2048, seq), tm, min(128, seq)


def kernel(x, e_norm, e_w_in, e_w_out, s5_lambda_re, s5_lambda_im, s5_log_dt, s5_b_re, s5_b_im, s5_c_re, s5_c_im, s5_d, s5_w_glu, s5_b_glu, da_q_norm, da_k_norm, da_lambda_q1, da_lambda_k1, da_lambda_q2, da_lambda_k2, da_out_norm, o_norm, o_w_in, o_b_f, o_w_out, fox_q_norm, fox_k_norm):
    batch, seq, d_model = x.shape
    tiles = _tiles(seq)
    depth = e_norm.shape[0] + o_norm.shape[0]
    x2d = x.reshape(batch * seq, d_model)
    for layer in range(depth):
        j = layer // 2
        if layer % 2 == 0:
            x2d = _even_layer(
                x2d, layer, batch, seq, tiles, e_norm[j], e_w_in[j], e_w_out[j],
                s5_lambda_re[j], s5_lambda_im[j], s5_log_dt[j], s5_b_re[j], s5_b_im[j],
                s5_c_re[j], s5_c_im[j], s5_d[j], s5_w_glu[j], s5_b_glu[j],
                da_q_norm[j], da_k_norm[j], da_lambda_q1[j], da_lambda_k1[j],
                da_lambda_q2[j], da_lambda_k2[j], da_out_norm[j])
        else:
            x2d = _odd_layer(x2d, batch, seq, tiles, o_norm[j], o_w_in[j], o_b_f[j], o_w_out[j],
                             fox_q_norm[j], fox_k_norm[j])
    return x2d.reshape(batch, seq, d_model)
```

```python
import functools
import math

import jax
import jax.numpy as jnp
from jax import lax
from jax.experimental import pallas as pl
from jax.experimental.pallas import tpu as pltpu

F32 = jnp.float32
BF16 = jnp.bfloat16
EPS = 1e-6
NEG = -1e30
LOG2E = math.log2(math.e)

LANES = 128
BF16_ROWS = 16
AUG = 256
CHUNK = 256
SCORES_AHEAD = 3
SEC = 1024
VMEM_LIMIT = 52 * 1024 * 1024

S5_GROUPS = 64
S5_GROUP = 16
S5_STATE = 64
S5_SLICES = 8
DA_HEADS = 8
DA_HEAD_DIM = 64
FOX_HEADS = 16
HEAD_W = 128
ACC_ROWS = HEAD_W + BF16_ROWS


def _params(sem):
    return pltpu.CompilerParams(dimension_semantics=sem, vmem_limit_bytes=VMEM_LIMIT)


def _head_norm_t(acc_t, gain_ref, head_dim):
    tm = acc_t.shape[1]
    r = acc_t.reshape(SEC // head_dim, head_dim, tm)
    ms = jnp.mean(r * r, axis=1, keepdims=True)
    n = (r * lax.rsqrt(ms + EPS)).reshape(SEC, tm)
    g = gain_ref[...]
    return jnp.concatenate(
        [n[:, c * LANES:(c + 1) * LANES] * g for c in range(tm // LANES)], axis=1)


def _proj_kernel(*refs, layout, head_dim, has_logf):
    it = iter(refs)
    x_ref, g_ref, w_ref = next(it), next(it), next(it)
    if has_logf:
        wf_ref, bf_ref = next(it), next(it)
    qg_ref, kg_ref = next(it), next(it)
    out_refs = [next(it) for _ in layout]
    if has_logf:
        logf_ref = next(it)
    hs_ref = next(it)

    j = pl.program_id(1)

    @pl.when(j == 0)
    def _():
        x = x_ref[...]
        ms = jnp.mean(x * x, axis=-1, keepdims=True)
        hs_ref[...] = (x * lax.rsqrt(ms + EPS) * g_ref[...]).astype(BF16)
        if has_logf:
            f = jnp.dot(hs_ref[...], wf_ref[...], preferred_element_type=F32) + bf_ref[...]
            logf_ref[...] = jax.nn.log_sigmoid(f)

    acc = jnp.dot(hs_ref[...], w_ref[...], preferred_element_type=F32)

    for (kind, j0, cnt), o_ref in zip(layout, out_refs):
        @pl.when((j >= j0) & (j < j0 + cnt))
        def _(kind=kind, o_ref=o_ref):
            if kind == "f32":
                o_ref[...] = acc
            elif kind == "vT":
                o_ref[...] = acc.T.astype(BF16)
            elif kind == "qT":
                o_ref[...] = _head_norm_t(acc.T, qg_ref, head_dim).astype(BF16)
            else:
                o_ref[...] = _head_norm_t(acc.T, kg_ref, head_dim).T.astype(BF16)


def _project(x2d, norm_g, w, qgain, kgain, *, batch, seq, tm, layout, head_dim, wf=None, bf=None):
    n_rows, d = x2d.shape
    nsec = w.shape[1] // SEC
    nt = seq // tm
    has_logf = wf is not None

    def sec(j, j0, cnt):
        return jnp.clip(j - j0, 0, cnt - 1)

    in_specs = [
        pl.BlockSpec((tm, d), lambda i, j: (i, 0)),
        pl.BlockSpec((1, d), lambda i, j: (0, 0)),
        pl.BlockSpec((d, SEC), lambda i, j: (0, j)),
    ]
    args = [x2d, norm_g.reshape(1, d), w]
    if has_logf:
        in_specs += [pl.BlockSpec((d, LANES), lambda i, j: (0, 0)),
                     pl.BlockSpec((1, LANES), lambda i, j: (0, 0))]
        args += [wf, bf]
    in_specs += [pl.BlockSpec((SEC, LANES), lambda i, j: (0, 0)),
                 pl.BlockSpec((SEC, LANES), lambda i, j: (0, 0))]
    args += [qgain, kgain]

    out_shapes, out_specs = [], []
    for kind, j0, cnt in layout:
        if kind == "f32" or kind == "k":
            dt = F32 if kind == "f32" else BF16
            out_shapes.append(jax.ShapeDtypeStruct((n_rows, cnt * SEC), dt))
            out_specs.append(pl.BlockSpec(
                (tm, SEC), lambda i, j, j0=j0, cnt=cnt: (i, sec(j, j0, cnt))))
        elif kind == "qT":
            out_shapes.append(jax.ShapeDtypeStruct((batch, cnt * SEC, seq), BF16))
            out_specs.append(pl.BlockSpec(
                (None, SEC, tm), lambda i, j, j0=j0, cnt=cnt: (i // nt, sec(j, j0, cnt), i % nt)))
        else:
            out_shapes.append(jax.ShapeDtypeStruct((batch, nt, cnt * SEC, tm), BF16))
            out_specs.append(pl.BlockSpec(
                (None, None, SEC, tm),
                lambda i, j, j0=j0, cnt=cnt: (i // nt, i % nt, sec(j, j0, cnt), 0)))
    if has_logf:
        out_shapes.append(jax.ShapeDtypeStruct((n_rows, LANES), F32))
        out_specs.append(pl.BlockSpec((tm, LANES), lambda i, j: (i, 0)))

    return pl.pallas_call(
        functools.partial(_proj_kernel, layout=layout, head_dim=head_dim, has_logf=has_logf),
        grid=(n_rows // tm, nsec),
        in_specs=in_specs,
        out_specs=out_specs,
        out_shape=out_shapes,
        scratch_shapes=[pltpu.VMEM((tm, d), BF16)],
        compiler_params=_params(("arbitrary", "arbitrary")),
        name="rms_in_proj",
    )(*args)


def _out_kernel(*refs, n_in):
    ys, ws = refs[:n_in], refs[n_in:2 * n_in]
    x_ref, o_ref = refs[2 * n_in], refs[2 * n_in + 1]
    acc = x_ref[...]
    for y_ref, w_ref in zip(ys, ws):
        acc = acc + jnp.dot(y_ref[...], w_ref[...], preferred_element_type=F32)
    o_ref[...] = acc


def _out_project(ys, ws, x2d, *, tm):
    n_rows, d = x2d.shape
    n_in = len(ys)
    in_specs = [pl.BlockSpec((tm, y.shape[1]), lambda i: (i, 0)) for y in ys]
    in_specs += [pl.BlockSpec(w.shape, lambda i: (0, 0)) for w in ws]
    in_specs += [pl.BlockSpec((tm, d), lambda i: (i, 0))]
    return pl.pallas_call(
        functools.partial(_out_kernel, n_in=n_in),
        grid=(n_rows // tm,),
        in_specs=in_specs,
        out_specs=pl.BlockSpec((tm, d), lambda i: (i, 0)),
        out_shape=jax.ShapeDtypeStruct((n_rows, d), F32),
        compiler_params=_params(("arbitrary",)),
        name="out_proj_residual",
    )(*ys, *ws, x2d)


def _s5_prepare(lam_re, lam_im, log_dt, b_re, b_im, c_re, c_im):
    dt = jnp.exp(log_dt.astype(F32))[:, None]
    lr, li = lam_re.astype(F32), lam_im.astype(F32)
    mag = jnp.exp(lr * dt)
    ar, ai = mag * jnp.cos(li * dt), mag * jnp.sin(li * dt)
    den = lr * lr + li * li
    nr, ni = ar - 1.0, ai
    kr = (nr * lr + ni * li) / den
    ki = (ni * lr - nr * li) / den
    br, bi = b_re.astype(F32), b_im.astype(F32)
    bbr = kr[..., None] * br - ki[..., None] * bi
    bbi = kr[..., None] * bi + ki[..., None] * br
    nblk = S5_GROUPS // 16
    eye = jnp.eye(16, dtype=F32)

    def b_tiles(bb):
        t = bb.reshape(nblk, 16, S5_SLICES, 8, S5_GROUP)
        return jnp.einsum("ngkph,fg->knfhgp", t, eye).reshape(S5_SLICES, nblk, 256, LANES)

    def c_tiles(cc):
        t = cc.reshape(nblk, 16, S5_GROUP, S5_SLICES, 8)
        return jnp.einsum("nghkp,fg->nkgpfh", t, eye).reshape(nblk, S5_SLICES, LANES, 256)

    bmat = jnp.concatenate([b_tiles(bbr), b_tiles(bbi)], axis=-1).astype(BF16)
    cmat = jnp.concatenate([c_tiles(c_re.astype(F32)), c_tiles(-c_im.astype(F32))], axis=2)
    cmat = cmat.reshape(nblk, S5_SLICES * 256, 256).astype(BF16)

    def a_tiles(a):
        t = a.reshape(nblk, 16, S5_SLICES, 8).transpose(2, 0, 1, 3)
        return t.reshape(S5_SLICES, nblk * LANES)

    return bmat, cmat, a_tiles(ar), a_tiles(ai)


def _s5_kernel(u_ref, z_ref, bm_ref, cm_ref, ar_ref, ai_ref, d_ref, wg_ref, bg_ref,
               o_ref, s_ref, h_ref, *, chunk, pitch):
    nb = u_ref.shape[0]
    nblk = ar_ref.shape[1] // LANES

    @pl.when(pl.program_id(0) == 0)
    def _():
        h_ref[...] = jnp.zeros_like(h_ref)

    for b in range(nb):
        ub = u_ref[b].astype(BF16)
        for k in range(S5_SLICES):
            for n in range(nblk):
                x = jnp.dot(ub[:, n * 256:(n + 1) * 256], bm_ref[k, n],
                            preferred_element_type=F32)
                s_ref[b, 2 * n, k * pitch:k * pitch + chunk, :] = x[:, :LANES]
                s_ref[b, 2 * n + 1, k * pitch:k * pitch + chunk, :] = x[:, LANES:]

    ar = [ar_ref[:, n * LANES:(n + 1) * LANES] for n in range(nblk)]
    ai = [ai_ref[:, n * LANES:(n + 1) * LANES] for n in range(nblk)]

    def step(t, hs):
        out = []
        for b in range(nb):
            for n in range(nblk):
                hr, hi = hs[2 * (b * nblk + n)], hs[2 * (b * nblk + n) + 1]
                rows = pl.ds(t, S5_SLICES, stride=pitch)
                xr = s_ref[b, 2 * n, rows, :]
                xi = s_ref[b, 2 * n + 1, rows, :]
                nr = ar[n] * hr - ai[n] * hi + xr
                ni = ar[n] * hi + ai[n] * hr + xi
                s_ref[b, 2 * n, rows, :] = nr
                s_ref[b, 2 * n + 1, rows, :] = ni
                out += [nr, ni]
        return tuple(out)

    h0 = tuple(h_ref[b, s] for b in range(nb) for s in range(2 * nblk))
    hs = lax.fori_loop(0, chunk, step, h0, unroll=4)
    for b in range(nb):
        for s in range(2 * nblk):
            h_ref[b, s] = hs[b * 2 * nblk + s]

    for b in range(nb):
        ys = []
        for n in range(nblk):
            acc = None
            for k in range(S5_SLICES):
                rows = slice(k * pitch, k * pitch + chunk)
                hk = jnp.concatenate([s_ref[b, 2 * n, rows, :], s_ref[b, 2 * n + 1, rows, :]],
                                     axis=1).astype(BF16)
                part = jnp.dot(hk, cm_ref[n, k * 256:(k + 1) * 256, :],
                               preferred_element_type=F32)
                acc = part if acc is None else acc + part
            ys.append(acc)
        y = jnp.concatenate(ys, axis=1) + d_ref[...] * u_ref[b]
        y = jax.nn.gelu(y)
        gate = jax.nn.sigmoid(
            jnp.dot(y.astype(BF16), wg_ref[...], preferred_element_type=F32) + bg_ref[...])
        o_ref[b] = (y * gate * jax.nn.silu(z_ref[b])).astype(BF16)


def _s5_mixer(u, z, bmat, cmat, a_re, a_im, d, w_glu, b_glu, *, chunk):
    batch, seq, width = u.shape
    pitch = chunk + 8
    nblk = width // 256
    const = lambda shape: pl.BlockSpec(shape, lambda c: (0,) * len(shape),
                                       pipeline_mode=pl.Buffered(1))
    return pl.pallas_call(
        functools.partial(_s5_kernel, chunk=chunk, pitch=pitch),
        grid=(seq // chunk,),
        in_specs=[
            pl.BlockSpec((batch, chunk, width), lambda c: (0, c, 0)),
            pl.BlockSpec((batch, chunk, width), lambda c: (0, c, 0)),
            const(bmat.shape), const(cmat.shape), const(a_re.shape), const(a_im.shape),
            const((1, width)), const(w_glu.shape), const((1, width)),
        ],
        out_specs=pl.BlockSpec((batch, chunk, width), lambda c: (0, c, 0)),
        out_shape=jax.ShapeDtypeStruct((batch, seq, width), BF16),
        scratch_shapes=[
            pltpu.VMEM((batch, 2 * nblk, S5_SLICES * pitch, LANES), F32),
            pltpu.VMEM((batch, 2 * nblk, S5_SLICES, LANES), F32),
        ],
        compiler_params=_params(("arbitrary",)),
        name="s5_mixer",
    )(u, z, bmat, cmat, a_re, a_im, d.reshape(1, width), w_glu, b_glu.reshape(1, width))


def _split3(x):
    hi = x.astype(BF16).astype(F32)
    r = x - hi
    mid = r.astype(BF16).astype(F32)
    return hi, mid, r - mid


def _select_rows(index, values):
    out = jnp.zeros(index.shape, F32)
    for i, v in enumerate(values):
        out = jnp.where(index == i, v, out)
    return out


class _Chain:
    def __init__(self, kaug_blk, vt_blk, qaug_ref, stats, c, mask_off):
        self.kaug_blk, self.vt_blk, self.qaug_ref = kaug_blk, vt_blk, qaug_ref
        self.m_ref, self.acc_ref = stats
        self.lanes = slice(c * CHUNK, (c + 1) * CHUNK)
        self.mask_off = mask_off

    def scores(self):
        s = jnp.dot(self.kaug_blk, self.qaug_ref[:, self.lanes],
                    preferred_element_type=F32)
        if self.mask_off is not None:
            key = lax.broadcasted_iota(jnp.int32, s.shape, 0)
            qry = lax.broadcasted_iota(jnp.int32, s.shape, 1)
            s = jnp.where(key - qry <= self.mask_off, s, NEG)
        return s

    def softmax(self, t):
        m_prev = self.m_ref[:, self.lanes]
        m_new = jnp.maximum(m_prev, jnp.max(t, axis=0, keepdims=True))
        alpha = jnp.exp2(m_prev - m_new)
        p = jnp.exp2(t - m_new)
        self.m_ref[:, self.lanes] = m_new
        return p.astype(BF16), alpha

    def values(self, p, alpha):
        pv = jnp.dot(self.vt_blk, p, preferred_element_type=F32)
        self.acc_ref[:, self.lanes] = alpha * self.acc_ref[:, self.lanes] + pv


def _run_chains(chains):
    ready = [ch.scores() for ch in chains[:SCORES_AHEAD]]
    for i, ch in enumerate(chains):
        if i + SCORES_AHEAD < len(chains):
            ready.append(chains[i + SCORES_AHEAD].scores())
        p, alpha = ch.softmax(ready.pop(0))
        ch.values(p, alpha)


def _causal_sweep(qi, tq, tk, load, make):
    per, nch = tq // tk, tq // CHUNK

    def body(j, carry):
        chains = []
        for d in range(per):
            blk = load(j * per + d)
            for c in range(nch):
                chains += make(blk, c, None)
        _run_chains(chains)
        return carry

    lax.fori_loop(0, qi, body, 0)
    chains = []
    for d in range(per):
        blk = load(qi * per + d)
        for c in range(nch):
            off = c * CHUNK - d * tk
            if off < -(CHUNK - 1):
                continue
            chains += make(blk, c, None if off >= tk - 1 else off)
    _run_chains(chains)


def _init_stats(*pairs):
    for m_ref, a_ref in pairs:
        m_ref[...] = jnp.full_like(m_ref, NEG)
        a_ref[...] = jnp.zeros_like(a_ref)


def _values_with_ones(vt_blk):
    row = lax.broadcasted_iota(jnp.int32, (BF16_ROWS, vt_blk.shape[1]), 0)
    ones = jnp.where(row == 0, 1.0, 0.0).astype(BF16)
    return jnp.concatenate([vt_blk, ones], axis=0)


def _normalised(acc_ref):
    return acc_ref[:HEAD_W, :] * (1.0 / acc_ref[HEAD_W:HEAD_W + 1, :])


def _fox_kernel(qt_ref, k_ref, vt_ref, c_ref, z_ref, o_ref,
                kaug_ref, qaug_ref, m_ref, acc_ref, *, tq, tk):
    qi = pl.program_id(2)

    @pl.when(qi == 0)
    def _():
        lane = lax.broadcasted_iota(jnp.int32, (LANES, LANES), 1)

        def fill(r, carry):
            rows = pl.ds(pl.multiple_of(r * LANES, LANES), LANES)
            col = jnp.broadcast_to(c_ref[pl.ds(r, 1), :], (LANES, LANES)).T
            hi, mid, lo = _split3(col * (-LOG2E))
            ext = _select_rows(lane, (hi, mid, lo, 1.0, 1.0, 1.0))
            kaug_ref[rows, :LANES] = k_ref[rows, :]
            kaug_ref[rows, LANES:] = ext.astype(BF16)
            return carry

        lax.fori_loop(0, c_ref.shape[0], fill, 0, unroll=8)

    crow = c_ref[pl.ds(qi * (tq // LANES), 1), :]
    c0 = jnp.broadcast_to(crow[:, 0:1], (BF16_ROWS, tq)) * LOG2E
    hi, mid, lo = _split3(c0)
    row = lax.broadcasted_iota(jnp.int32, (BF16_ROWS, tq), 0)
    qaug_ref[:HEAD_W, :] = qt_ref[...]
    qaug_ref[HEAD_W:HEAD_W + BF16_ROWS, :] = _select_rows(row, (1.0, 1.0, 1.0, hi, mid, lo)).astype(BF16)
    qaug_ref[HEAD_W + BF16_ROWS:, :] = jnp.zeros((AUG - HEAD_W - BF16_ROWS, tq), BF16)
    _init_stats((m_ref, acc_ref))

    def load(kb):
        ks = pl.multiple_of(kb * tk, tk)
        return kaug_ref[pl.ds(ks, tk), :], _values_with_ones(vt_ref[kb])

    def make(blk, c, mask_off):
        return [_Chain(blk[0], blk[1], qaug_ref, (m_ref, acc_ref), c, mask_off)]

    _causal_sweep(qi, tq, tk, load, make)

    o = _normalised(acc_ref).T
    o_ref[...] = (o * jax.nn.silu(z_ref[...])).astype(BF16)


def _fox_attention(qt, k, vt, c2, z, *, tq, tk):
    batch, width, seq = qt.shape
    heads = width // HEAD_W
    return pl.pallas_call(
        functools.partial(_fox_kernel, tq=tq, tk=tk),
        grid=(batch, heads, seq // tq),
        in_specs=[
            pl.BlockSpec((None, HEAD_W, tq), lambda b, h, q: (b, h, q)),
            pl.BlockSpec((None, seq, HEAD_W), lambda b, h, q: (b, 0, h)),
            pl.BlockSpec((None, seq // tk, HEAD_W, tk), lambda b, h, q: (b, 0, h, 0)),
            pl.BlockSpec((None, None, seq // LANES, LANES), lambda b, h, q: (b, h, 0, 0)),
            pl.BlockSpec((None, tq, HEAD_W), lambda b, h, q: (b, q, h)),
        ],
        out_specs=pl.BlockSpec((None, tq, HEAD_W), lambda b, h, q: (b, q, h)),
        out_shape=jax.ShapeDtypeStruct((batch, seq, width), BF16),
        scratch_shapes=[
            pltpu.VMEM((seq, AUG), BF16), pltpu.VMEM((AUG, tq), BF16),
            pltpu.VMEM((1, tq), F32), pltpu.VMEM((ACC_ROWS, tq), F32),
        ],
        compiler_params=_params(("arbitrary", "arbitrary", "arbitrary")),
        name="fox_attention",
    )(qt, k, vt, c2, z)


def _diff_kernel(qt_ref, k_ref, vt_ref, hp_ref, og_ref, z_ref, o_ref,
                 kaug_ref, qa0_ref, qa1_ref, m0_ref, a0_ref, m1_ref, a1_ref,
                 *, tq, tk, out_scale):
    qi = pl.program_id(2)
    dh = qt_ref.shape[0] // 2
    slope2 = hp_ref[0:1, :] * LOG2E
    lam = hp_ref[1:2, 0:1]

    @pl.when(qi == 0)
    def _():
        lane = lax.broadcasted_iota(jnp.int32, (LANES, LANES), 1)
        key = lax.broadcasted_iota(jnp.int32, (LANES, LANES), 0)

        def fill(r, carry):
            rows = pl.ds(pl.multiple_of(r * LANES, LANES), LANES)
            base = (key * 0 + r * LANES).astype(F32)
            ext = _select_rows(lane, _split3(slope2 * base) + _split3(slope2 * key.astype(F32))
                               + (1.0, 1.0, 1.0))
            kaug_ref[rows, :LANES] = k_ref[rows, :]
            kaug_ref[rows, LANES:] = ext.astype(BF16)
            return carry

        lax.fori_loop(0, k_ref.shape[0] // LANES, fill, 0)

    row = lax.broadcasted_iota(jnp.int32, (BF16_ROWS, tq), 0)
    q0 = (row * 0 + qi * tq).astype(F32)
    slope_t = jnp.concatenate([slope2] * (tq // LANES), axis=1)
    ext = _select_rows(row, (1.0,) * 6 + _split3(-slope_t * q0)).astype(BF16)
    qt = qt_ref[...]
    first = lax.broadcasted_iota(jnp.int32, qt.shape, 0) < dh
    zero = jnp.zeros_like(qt)
    for qa_ref, comp in ((qa0_ref, jnp.where(first, qt, zero)), (qa1_ref, jnp.where(first, zero, qt))):
        qa_ref[:HEAD_W, :] = comp
        qa_ref[HEAD_W:HEAD_W + BF16_ROWS, :] = ext
        qa_ref[HEAD_W + BF16_ROWS:, :] = jnp.zeros((AUG - HEAD_W - BF16_ROWS, tq), BF16)
    _init_stats((m0_ref, a0_ref), (m1_ref, a1_ref))

    def load(kb):
        ks = pl.multiple_of(kb * tk, tk)
        return kaug_ref[pl.ds(ks, tk), :], _values_with_ones(vt_ref[kb])

    def make(blk, c, mask_off):
        return [_Chain(blk[0], blk[1], qa0_ref, (m0_ref, a0_ref), c, mask_off),
                _Chain(blk[0], blk[1], qa1_ref, (m1_ref, a1_ref), c, mask_off)]

    _causal_sweep(qi, tq, tk, load, make)

    ot = _normalised(a0_ref) - lam * _normalised(a1_ref)
    ms = jnp.mean(ot * ot, axis=0, keepdims=True)
    o = (ot * lax.rsqrt(ms + EPS)).T
    o = o * og_ref[...] * out_scale
    o_ref[...] = (o * jax.nn.silu(z_ref[...])).astype(BF16)


def _diff_attention(qt, k, vt, head_params, out_gain, z, *, tq, tk, out_scale):
    batch, width, seq = qt.shape
    heads = width // HEAD_W
    stat = lambda: pltpu.VMEM((1, tq), F32)
    accum = lambda: pltpu.VMEM((ACC_ROWS, tq), F32)
    return pl.pallas_call(
        functools.partial(_diff_kernel, tq=tq, tk=tk, out_scale=out_scale),
        grid=(batch, heads, seq // tq),
        in_specs=[
            pl.BlockSpec((None, HEAD_W, tq), lambda b, h, q: (b, h, q)),
            pl.BlockSpec((None, seq, HEAD_W), lambda b, h, q: (b, 0, h)),
            pl.BlockSpec((None, seq // tk, HEAD_W, tk), lambda b, h, q: (b, 0, h, 0)),
            pl.BlockSpec((None, 8, LANES), lambda b, h, q: (h, 0, 0)),
            pl.BlockSpec((1, HEAD_W), lambda b, h, q: (0, 0)),
            pl.BlockSpec((None, tq, HEAD_W), lambda b, h, q: (b, q, h)),
        ],
        out_specs=pl.BlockSpec((None, tq, HEAD_W), lambda b, h, q: (b, q, h)),
        out_shape=jax.ShapeDtypeStruct((batch, seq, width), BF16),
        scratch_shapes=[pltpu.VMEM((seq, AUG), BF16),
                        pltpu.VMEM((AUG, tq), BF16), pltpu.VMEM((AUG, tq), BF16),
                        stat(), accum(), stat(), accum()],
        compiler_params=_params(("arbitrary", "arbitrary", "arbitrary")),
        name="diff_attention",
    )(qt, k, vt, head_params, out_gain, z)


def _cumsum_kernel(lf_ref, o_ref, carry_ref, *, rows):
    @pl.when(pl.program_id(1) == 0)
    def _():
        carry_ref[...] = jnp.zeros_like(carry_ref)

    x = lf_ref[...]
    r = lax.broadcasted_iota(jnp.int32, (rows, rows), 0)
    c = lax.broadcasted_iota(jnp.int32, (rows, rows), 1)
    tri = (c <= r).astype(BF16)
    hi = x.astype(BF16)
    r1 = x - hi.astype(F32)
    mid = r1.astype(BF16)
    lo = (r1 - mid.astype(F32)).astype(BF16)
    cum = (jnp.dot(tri, hi, preferred_element_type=F32)
           + jnp.dot(tri, mid, preferred_element_type=F32)
           + jnp.dot(tri, lo, preferred_element_type=F32)) + carry_ref[...]
    carry_ref[...] = cum[rows - 1:rows, :]
    o_ref[...] = cum.T


def _cumsum_t(logf, *, rows):
    batch, seq, _ = logf.shape
    return pl.pallas_call(
        functools.partial(_cumsum_kernel, rows=rows),
        grid=(batch, seq // rows),
        in_specs=[pl.BlockSpec((None, rows, LANES), lambda b, i: (b, i, 0))],
        out_specs=pl.BlockSpec((None, LANES, rows), lambda b, i: (b, 0, i)),
        out_shape=jax.ShapeDtypeStruct((batch, LANES, seq), F32),
        scratch_shapes=[pltpu.VMEM((1, LANES), F32)],
        compiler_params=_params(("arbitrary", "arbitrary")),
        name="logf_cumsum",
    )(logf)


def _lane_col(v):
    return jnp.broadcast_to(v.astype(F32)[:, None], (v.shape[0], LANES))


def _alibi_slopes(n):
    return jnp.asarray([2.0 ** (-8.0 * (h + 1) / n) for h in range(n)], F32)


def _even_layer(x2d, layer, batch, seq, tiles, e_norm, e_w_in, e_w_out, lam_re, lam_im, log_dt,
                b_re, b_im, c_re, c_im, d, w_glu, b_glu, q_norm, k_norm, lq1, lk1, lq2, lk2,
                out_norm):
    tm, tq, tk, chunk = tiles
    s5w = S5_GROUPS * S5_GROUP
    lambda_init = 0.8 - 0.6 * math.exp(-0.3 * layer)
    qgain = _lane_col(jnp.tile(q_norm.astype(F32) * (DA_HEAD_DIM ** -0.5 * LOG2E), DA_HEADS))
    kgain = _lane_col(jnp.tile(k_norm, DA_HEADS))
    layout = (("f32", 0, 1), ("f32", 1, 1), ("qT", 2, 1), ("k", 3, 1), ("vT", 4, 1), ("f32", 5, 1))
    u, zs, qt, k, vt, zd = _project(
        x2d, e_norm, e_w_in.astype(BF16), qgain, kgain, batch=batch, seq=seq, tm=tm,
        layout=layout, head_dim=DA_HEAD_DIM)

    bmat, cmat, a_re, a_im = _s5_prepare(lam_re, lam_im, log_dt, b_re, b_im, c_re, c_im)
    y_s5 = _s5_mixer(u.reshape(batch, seq, s5w), zs.reshape(batch, seq, s5w), bmat, cmat,
                     a_re, a_im, d, w_glu.astype(BF16), b_glu, chunk=chunk)

    lam = (jnp.exp(jnp.sum(lq1.astype(F32) * lk1.astype(F32)))
           - jnp.exp(jnp.sum(lq2.astype(F32) * lk2.astype(F32))) + lambda_init)
    head_params = jnp.zeros((DA_HEADS, 8, LANES), F32)
    head_params = head_params.at[:, 0, :].set(_alibi_slopes(DA_HEADS)[:, None])
    head_params = head_params.at[:, 1, :].set(lam)
    y_da = _diff_attention(qt, k.reshape(batch, seq, -1), vt, head_params,
                           out_norm.astype(F32).reshape(1, HEAD_W),
                           zd.reshape(batch, seq, -1), tq=tq, tk=tk,
                           out_scale=1.0 - lambda_init)

    w_out = e_w_out.astype(BF16)
    return _out_project([y_s5.reshape(batch * seq, s5w), y_da.reshape(batch * seq, -1)],
                        [w_out[:s5w], w_out[s5w:]], x2d, tm=tm)


def _odd_layer(x2d, batch, seq, tiles, o_norm, o_w_in, o_b_f, o_w_out, q_norm, k_norm):
    tm, tq, tk, _ = tiles
    width = FOX_HEADS * HEAD_W
    nsec = width // SEC
    qgain = _lane_col(jnp.tile(q_norm.astype(F32) * (HEAD_W ** -0.5 * LOG2E), SEC // HEAD_W))
    kgain = _lane_col(jnp.tile(k_norm, SEC // HEAD_W))
    w = o_w_in.astype(BF16)
    wf = jnp.zeros((w.shape[0], LANES), BF16).at[:, :FOX_HEADS].set(w[:, 4 * width:])
    bf = jnp.zeros((1, LANES), F32).at[0, :FOX_HEADS].set(o_b_f.astype(F32))
    layout = (("qT", 0, nsec), ("k", nsec, nsec), ("vT", 2 * nsec, nsec), ("f32", 3 * nsec, nsec))
    qt, k, vt, z, logf = _project(
        x2d, o_norm, w[:, :4 * width], qgain, kgain, batch=batch, seq=seq, tm=tm,
        layout=layout, head_dim=HEAD_W, wf=wf, bf=bf)
    ct = _cumsum_t(logf.reshape(batch, seq, LANES), rows=tm)
    c2 = ct[:, :FOX_HEADS].reshape(batch, FOX_HEADS, seq // LANES, LANES)
    y = _fox_attention(qt, k.reshape(batch, seq, width), vt, c2, z.reshape(batch, seq, width),
                       tq=min(2 * tq, seq), tk=tk)
    return _out_project([y.reshape(batch * seq, width)], [o_w_out.astype(BF16)], x2d, tm=tm)


def _tiles(seq):
    tm = min(512, seq)
    return tm, min(2048, seq), tm, min(256, seq)


def kernel(x, e_norm, e_w_in, e_w_out, s5_lambda_re, s5_lambda_im, s5_log_dt, s5_b_re, s5_b_im, s5_c_re, s5_c_im, s5_d, s5_w_glu, s5_b_glu, da_q_norm, da_k_norm, da_lambda_q1, da_lambda_k1, da_lambda_q2, da_lambda_k2, da_out_norm, o_norm, o_w_in, o_b_f, o_w_out, fox_q_norm, fox_k_norm):
    batch, seq, d_model = x.shape
    tiles = _tiles(seq)
    depth = e_norm.shape[0] + o_norm.shape[0]
    x2d = x.reshape(batch * seq, d_model)
    for layer in range(depth):
        j = layer // 2
        if layer % 2 == 0:
            x2d = _even_layer(
                x2d, layer, batch, seq, tiles, e_norm[j], e_w_in[j], e_w_out[j],
                s5_lambda_re[j], s5_lambda_im[j], s5_log_dt[j], s5_b_re[j], s5_b_im[j],
                s5_c_re[j], s5_c_im[j], s5_d[j], s5_w_glu[j], s5_b_glu[j],
                da_q_norm[j], da_k_norm[j], da_lambda_q1[j], da_lambda_k1[j],
                da_lambda_q2[j], da_lambda_k2[j], da_out_norm[j])
        else:
            x2d = _odd_layer(x2d, batch, seq, tiles, o_norm[j], o_w_in[j], o_b_f[j], o_w_out[j],
                             fox_q_norm[j], fox_k_norm[j])
    return x2d.reshape(batch, seq, d_model)
```
